```python
import math
import jax, jax.numpy as jnp
from jax import lax
import numpy as np

D_MODEL = 1024
BATCH = 8
SEQ = 8192
DEPTH = 1

FOX_HEADS = 8
FOX_HEAD_DIM = 64
FOX_WIDTH = FOX_HEADS * FOX_HEAD_DIM
DIFF_HEADS = 4
DIFF_QK_DIM = 64
DIFF_V_DIM = 2 * DIFF_QK_DIM
DIFF_WIDTH = DIFF_HEADS * DIFF_V_DIM
BRANCH_WIDTH = 512
N_BRANCHES = 2

BLOCK_Q = 128
ROPE_THETA = 10000.0
NORM_EPS = 1e-6
SUBLN_EPS = 1e-5

SPLIT_SIZES = [
    FOX_WIDTH,
    FOX_WIDTH,
    FOX_WIDTH,
    FOX_HEADS,
    FOX_WIDTH,
    DIFF_HEADS * 2 * DIFF_QK_DIM,
    DIFF_HEADS * 2 * DIFF_QK_DIM,
    DIFF_WIDTH,
    DIFF_WIDTH,
    N_BRANCHES * D_MODEL,
]
N_IN = sum(SPLIT_SIZES)
SPLIT_POINTS = [int(v) for v in np.cumsum(SPLIT_SIZES)[:-1]]

kernel_name = "hybrid_fox_diffattn_gated_merge"


def lambda_init_for(layer_idx):
    return 0.8 - 0.6 * math.exp(-0.3 * (layer_idx - 1))


def rms_norm(x, g, eps):
    xf = x.astype(jnp.float32)
    y = xf * lax.rsqrt(jnp.mean(xf * xf, axis=-1, keepdims=True) + eps)
    return (y * g.astype(jnp.float32)).astype(x.dtype)


def rope_tables(seq, dim):
    pos = jnp.arange(seq, dtype=jnp.float32)
    inv_freq = ROPE_THETA ** (-jnp.arange(0, dim, 2, dtype=jnp.float32) / dim)
    ang = pos[:, None] * inv_freq[None, :]
    return jnp.cos(ang), jnp.sin(ang)


def apply_rope(x, cos, sin):
    x1, x2 = jnp.split(x, 2, axis=-1)
    c = cos.astype(x.dtype)
    s = sin.astype(x.dtype)
    return jnp.concatenate([x1 * c - x2 * s, x2 * c + x1 * s], axis=-1)


def split_heads(t, n_heads):
    b, s, _ = t.shape
    return t.reshape(b, s, n_heads, -1).transpose(0, 2, 1, 3)


def merge_heads(t):
    b, h, s, d = t.shape
    return t.transpose(0, 2, 1, 3).reshape(b, s, h * d)


def fox_attention(q, k, v, log_f):
    b, h, s, d = q.shape
    nb = s // BLOCK_Q
    scale = 1.0 / math.sqrt(d)
    c = jnp.cumsum(log_f, axis=-1)
    qb = q.reshape(b, h, nb, BLOCK_Q, d).transpose(2, 0, 1, 3, 4)
    cb = c.reshape(b, h, nb, BLOCK_Q).transpose(2, 0, 1, 3)
    kpos = jnp.arange(s)

    def step(args):
        i, qi, ci = args
        qpos = i * BLOCK_Q + jnp.arange(BLOCK_Q)
        logits = jnp.einsum('bhqd,bhkd->bhqk', qi, k).astype(jnp.float32) * scale
        logits = logits + ci[..., :, None] - c[..., None, :]
        logits = jnp.where(kpos[None, :] <= qpos[:, None], logits, -jnp.inf)
        p = jax.nn.softmax(logits, axis=-1)
        return jnp.einsum('bhqk,bhkd->bhqd', p.astype(v.dtype), v)

    out = lax.map(step, (jnp.arange(nb), qb, cb))
    return out.transpose(1, 2, 0, 3, 4).reshape(b, h, s, d)


def diff_attention(q, k, v, lam):
    b, h, _, s, d = q.shape
    nb = s // BLOCK_Q
    scale = 1.0 / math.sqrt(d)
    qb = q.reshape(b, h, 2, nb, BLOCK_Q, d).transpose(3, 0, 1, 2, 4, 5)
    kpos = jnp.arange(s)

    def step(args):
        i, qi = args
        qpos = i * BLOCK_Q + jnp.arange(BLOCK_Q)
        logits = jnp.einsum('bhmqd,bhmkd->bhmqk', qi, k).astype(jnp.float32) * scale
        logits = jnp.where(kpos[None, :] <= qpos[:, None], logits, -jnp.inf)
        p = jax.nn.softmax(logits, axis=-1)
        pd = p[:, :, 0] - lam * p[:, :, 1]
        return jnp.einsum('bhqk,bhkd->bhqd', pd.astype(v.dtype), v)

    out = lax.map(step, (jnp.arange(nb), qb))
    return out.transpose(1, 2, 0, 3, 4).reshape(b, h, s, v.shape[-1])


def setup_inputs(seed: int = 0) -> dict:
    key = jax.random.key(seed)
    ks = jax.random.split(key, 13)
    f32 = jnp.float32
    x = jax.random.normal(ks[0], (BATCH, SEQ, D_MODEL), f32)
    g_pre = 1.0 + 0.1 * jax.random.normal(ks[1], (DEPTH, D_MODEL), f32)
    w_in = jax.random.normal(ks[2], (DEPTH, D_MODEL, N_IN), f32) * D_MODEL ** -0.5
    b_forget = jax.random.uniform(ks[3], (DEPTH, FOX_HEADS), f32, minval=1.0, maxval=5.0)
    lambda_q1 = 0.1 * jax.random.normal(ks[4], (DEPTH, DIFF_QK_DIM), f32)
    lambda_k1 = 0.1 * jax.random.normal(ks[5], (DEPTH, DIFF_QK_DIM), f32)
    lambda_q2 = 0.1 * jax.random.normal(ks[6], (DEPTH, DIFF_QK_DIM), f32)
    lambda_k2 = 0.1 * jax.random.normal(ks[7], (DEPTH, DIFF_QK_DIM), f32)
    g_subln = 1.0 + 0.1 * jax.random.normal(ks[8], (DEPTH, DIFF_V_DIM), f32)
    w_branch = jax.random.normal(ks[9], (DEPTH, N_BRANCHES, BRANCH_WIDTH, D_MODEL), f32) * BRANCH_WIDTH ** -0.5
    w_out = jax.random.normal(ks[10], (DEPTH, D_MODEL, D_MODEL), f32) * D_MODEL ** -0.5
    g_post = 1.0 + 0.1 * jax.random.normal(ks[11], (DEPTH, D_MODEL), f32)
    return {"x": x, "g_pre": g_pre, "w_in": w_in, "b_forget": b_forget,
            "lambda_q1": lambda_q1, "lambda_k1": lambda_k1,
            "lambda_q2": lambda_q2, "lambda_k2": lambda_k2,
            "g_subln": g_subln, "w_branch": w_branch, "w_out": w_out,
            "g_post": g_post}


def reference(x, g_pre, w_in, b_forget, lambda_q1, lambda_k1, lambda_q2, lambda_k2,
              g_subln, w_branch, w_out, g_post):
    b, s, _ = x.shape
    cos, sin = rope_tables(s, DIFF_QK_DIM)
    for l in range(DEPTH):
        lam_init = lambda_init_for(l + 1)
        h = rms_norm(x, g_pre[l], NORM_EPS)
        proj = jnp.einsum('bsd,dn->bsn', h, w_in[l])
        qa, ka, va, fa, za, qb, kb, vb, zb, gates = jnp.split(proj, SPLIT_POINTS, axis=-1)

        log_f = jax.nn.log_sigmoid((fa + b_forget[l]).astype(jnp.float32)).transpose(0, 2, 1)
        ya = fox_attention(split_heads(qa, FOX_HEADS), split_heads(ka, FOX_HEADS),
                           split_heads(va, FOX_HEADS), log_f)
        ya = merge_heads(ya) * jax.nn.silu(za)

        qd = apply_rope(qb.reshape(b, s, DIFF_HEADS, 2, DIFF_QK_DIM).transpose(0, 2, 3, 1, 4), cos, sin)
        kd = apply_rope(kb.reshape(b, s, DIFF_HEADS, 2, DIFF_QK_DIM).transpose(0, 2, 3, 1, 4), cos, sin)
        vd = split_heads(vb, DIFF_HEADS)
        lam = (jnp.exp(jnp.sum(lambda_q1[l].astype(jnp.float32) * lambda_k1[l].astype(jnp.float32)))
               - jnp.exp(jnp.sum(lambda_q2[l].astype(jnp.float32) * lambda_k2[l].astype(jnp.float32)))
               + lam_init)
        yb = diff_attention(qd, kd, vd, lam)
        yb = rms_norm(yb, g_subln[l], SUBLN_EPS) * (1.0 - lam_init)
        yb = merge_heads(yb) * jax.nn.silu(zb)

        gate_a, gate_b = jnp.split(jax.nn.sigmoid(gates), N_BRANCHES, axis=-1)
        merged = (gate_a * jnp.einsum('bsw,wd->bsd', ya, w_branch[l, 0])
                  + gate_b * jnp.einsum('bsw,wd->bsd', yb, w_branch[l, 1]))
        y = jnp.einsum('bsd,de->bse', merged, w_out[l])
        x = x + rms_norm(y, g_post[l], NORM_EPS)
    return x
```

```python
import functools
import math

import jax
import jax.numpy as jnp
from jax import lax
from jax.experimental import pallas as pl
from jax.experimental.pallas import tpu as pltpu

F32 = jnp.float32
BF16 = jnp.bfloat16

D_MODEL = 1024
FOX_HEADS = 8
FOX_HEAD_DIM = 64
DIFF_HEADS = 4
DIFF_QK_DIM = 64
DIFF_V_DIM = 128
WIDTH = 512
ROPE_THETA = 10000.0
NORM_EPS = 1e-6
SUBLN_EPS = 1e-5
LAMBDA_INIT = 0.8 - 0.6 * math.exp(-0.3 * 0)

LANES = 128
TM = 512
TQ = 512
TK = 512
N_SPLIT = 3
AUG0 = FOX_HEAD_DIM
QK_SCALE = 1.0 / math.sqrt(FOX_HEAD_DIM)
NEG_BIG = -1e30
VMEM_LIMIT = 48 * 1024 * 1024

_NT = (((1,), (1,)), ((), ()))
_TN = (((0,), (0,)), ((), ()))


def _const_spec(shape):
    return pl.BlockSpec(shape, lambda *_: (0,) * len(shape),
                        pipeline_mode=pl.Buffered(1))


def _rms_rows(x, g, eps):
    ms = jnp.mean(x * x, axis=-1, keepdims=True)
    return x * lax.rsqrt(ms + eps) * g


def _split_bf16(x):
    parts = []
    r = x
    for _ in range(N_SPLIT - 1):
        p = r.astype(BF16)
        parts.append(p)
        r = r - p.astype(F32)
    parts.append(r.astype(BF16))
    return parts


def _log_sigmoid(x):
    return jnp.minimum(x, 0.0) - jnp.log(1.0 + jnp.exp(-jnp.abs(x)))


def _silu(z):
    return z * (1.0 / (1.0 + jnp.exp(-z)))


def _sigmoid(z):
    return 1.0 / (1.0 + jnp.exp(-z))


def _fox_qk_kernel(tiles_per_seq, x_ref, g_ref, wq_ref, wk_ref, wf_ref, bf_ref,
                   place_ref, q_ref, k_ref, carry_ref):
    i = pl.program_id(0)

    @pl.when(i % tiles_per_seq == 0)
    def _():
        carry_ref[...] = jnp.zeros_like(carry_ref)

    h = _rms_rows(x_ref[...], g_ref[...], NORM_EPS).astype(BF16)

    q = jnp.dot(h, wq_ref[...], preferred_element_type=F32)
    lane = lax.broadcasted_iota(jnp.int32, (1, q.shape[1]), 1) % LANES
    ones_lane = (lane >= AUG0) & (lane < AUG0 + N_SPLIT)
    q = jnp.where(ones_lane, 1.0, q * QK_SCALE).astype(BF16)
    for hh in range(FOX_HEADS):
        q_ref[hh] = q[:, hh * LANES:(hh + 1) * LANES]

    f = jnp.dot(h, wf_ref[...], preferred_element_type=F32) + bf_ref[...]
    logf = _log_sigmoid(f)
    rows = lax.broadcasted_iota(jnp.int32, (TM, TM), 0)
    cols = lax.broadcasted_iota(jnp.int32, (TM, TM), 1)
    tri = (rows >= cols).astype(BF16)
    c = carry_ref[...]
    for part in _split_bf16(logf):
        c = c + jnp.dot(tri, part, preferred_element_type=F32)
    carry_ref[...] = c[TM - 1:TM, :]

    neg_c = jnp.concatenate(_split_bf16(-c), axis=1)
    k = (jnp.dot(h, wk_ref[...], preferred_element_type=F32)
         + jnp.dot(neg_c, place_ref[...], preferred_element_type=F32)).astype(BF16)
    for hh in range(FOX_HEADS):
        k_ref[hh] = k[:, hh * LANES:(hh + 1) * LANES]


def _rope(x, cos, sin_signed):
    lane = lax.broadcasted_iota(jnp.int32, (1, LANES), 1) % DIFF_QK_DIM
    first_half = lane < DIFF_QK_DIM // 2
    outs = []
    for g in range(x.shape[1] // LANES):
        xg = x[:, g * LANES:(g + 1) * LANES]
        partner = jnp.where(first_half,
                            pltpu.roll(xg, LANES - DIFF_QK_DIM // 2, axis=1),
                            pltpu.roll(xg, DIFF_QK_DIM // 2, axis=1))
        outs.append(xg * cos + partner * sin_signed)
    return jnp.concatenate(outs, axis=1)


def _nat_kernel(x_ref, g_ref, w_ref, cos_ref, sin_ref, q_ref, k_ref, z_ref, gate_ref):
    h = _rms_rows(x_ref[...], g_ref[...], NORM_EPS).astype(BF16)
    cos = cos_ref[...]
    sin = sin_ref[...]
    q = jnp.dot(h, w_ref[:, 0:WIDTH], preferred_element_type=F32)
    q_ref[...] = (_rope(q, cos, sin) * QK_SCALE).astype(BF16)
    k = jnp.dot(h, w_ref[:, WIDTH:2 * WIDTH], preferred_element_type=F32)
    k_ref[...] = _rope(k, cos, sin).astype(BF16)
    z_ref[...] = jnp.dot(h, w_ref[:, 2 * WIDTH:3 * WIDTH],
                         preferred_element_type=F32).astype(BF16)
    gate_ref[...] = jnp.dot(h, w_ref[:, 3 * WIDTH:],
                            preferred_element_type=F32).astype(BF16)


def _tr_kernel(x_ref, g_ref, wt_ref, vf_ref, zf_ref, vd_ref):
    h = _rms_rows(x_ref[...], g_ref[...], NORM_EPS).astype(BF16)
    for idx, ref in enumerate((vf_ref, zf_ref, vd_ref)):
        w = wt_ref[idx * WIDTH:(idx + 1) * WIDTH, :]
        ref[...] = lax.dot_general(w, h, _NT,
                                   preferred_element_type=F32).astype(BF16)


def _softmax_step(s_t, m_ref, l_ref):
    m_old = m_ref[...]
    m_new = jnp.maximum(m_old, jnp.max(s_t, axis=0, keepdims=True))
    alpha = jnp.exp(m_old - m_new)
    p = jnp.exp(s_t - m_new)
    l_ref[...] = alpha * l_ref[...] + jnp.sum(p, axis=0, keepdims=True)
    m_ref[...] = m_new
    return p, alpha


def _causal_mask(s_t):
    kpos = lax.broadcasted_iota(jnp.int32, s_t.shape, 0)
    qpos = lax.broadcasted_iota(jnp.int32, s_t.shape, 1)
    return jnp.where(kpos <= qpos, s_t, NEG_BIG)


def _fox_attn_kernel(q_ref, k_ref, vt_ref, zt_ref, o_ref, acc_ref, m_ref, l_ref):
    i = pl.program_id(2)
    q = q_ref[...]
    m_ref[...] = jnp.full_like(m_ref, NEG_BIG)
    l_ref[...] = jnp.zeros_like(l_ref)
    acc_ref[...] = jnp.zeros_like(acc_ref)

    def block(j, masked):
        k = k_ref[pl.ds(pl.multiple_of(j * TK, TK), TK), :]
        s_t = lax.dot_general(k, q, _NT, preferred_element_type=F32)
        if masked:
            s_t = _causal_mask(s_t)
        p, alpha = _softmax_step(s_t, m_ref, l_ref)
        pv = jnp.dot(vt_ref[j], p.astype(BF16), preferred_element_type=F32)
        acc_ref[...] = alpha * acc_ref[...] + pv

    def body(j, carry):
        block(j, False)
        return carry

    lax.fori_loop(0, i, body, 0)
    block(i, True)

    z = zt_ref[...].astype(F32)
    o_ref[...] = (acc_ref[...] * (1.0 / l_ref[...]) * _silu(z)).astype(BF16)


def _diff_attn_kernel(lamv_ref, q_ref, k_ref, vt_ref, z_ref, g_ref, o_ref,
                      acc1_ref, acc2_ref, m1_ref, m2_ref, l1_ref, l2_ref):
    i = pl.program_id(2)
    q = q_ref[...]
    lane = lax.broadcasted_iota(jnp.int32, (1, LANES), 1)
    zero = jnp.zeros_like(q)
    q1 = jnp.where(lane < DIFF_QK_DIM, q, zero)
    q2 = jnp.where(lane >= DIFF_QK_DIM, q, zero)
    for m_ref, l_ref, acc_ref in ((m1_ref, l1_ref, acc1_ref), (m2_ref, l2_ref, acc2_ref)):
        m_ref[...] = jnp.full_like(m_ref, NEG_BIG)
        l_ref[...] = jnp.zeros_like(l_ref)
        acc_ref[...] = jnp.zeros_like(acc_ref)

    def block(j, masked):
        k = k_ref[pl.ds(pl.multiple_of(j * TK, TK), TK), :]
        vt = vt_ref[j]
        for qm, m_ref, l_ref, acc_ref in ((q1, m1_ref, l1_ref, acc1_ref),
                                          (q2, m2_ref, l2_ref, acc2_ref)):
            s_t = lax.dot_general(k, qm, _NT, preferred_element_type=F32)
            if masked:
                s_t = _causal_mask(s_t)
            p, alpha = _softmax_step(s_t, m_ref, l_ref)
            pv = jnp.dot(vt, p.astype(BF16), preferred_element_type=F32)
            acc_ref[...] = alpha * acc_ref[...] + pv

    def body(j, carry):
        block(j, False)
        return carry

    lax.fori_loop(0, i, body, 0)
    block(i, True)

    lamv = lamv_ref[...]
    lam = (jnp.exp(jnp.sum(lamv[0:1] * lamv[1:2], axis=1, keepdims=True))
           - jnp.exp(jnp.sum(lamv[2:3] * lamv[3:4], axis=1, keepdims=True))
           + LAMBDA_INIT)
    o = (acc1_ref[...] * (1.0 / l1_ref[...])
         - lam * (acc2_ref[...] * (1.0 / l2_ref[...])))
    ms = jnp.mean(o * o, axis=0, keepdims=True)
    y = o * lax.rsqrt(ms + SUBLN_EPS) * g_ref[...] * (1.0 - LAMBDA_INIT)
    z = z_ref[...].astype(F32)
    o_ref[...] = (y.T * _silu(z)).astype(BF16)


def _merge_kernel(yat_ref, yb_ref, gate_ref, x_ref, wa_ref, wb_ref, wo_ref, g_ref, o_ref):
    ma = lax.dot_general(yat_ref[...], wa_ref[...], _TN, preferred_element_type=F32)
    mb = jnp.dot(yb_ref[...], wb_ref[...], preferred_element_type=F32)
    ga = _sigmoid(gate_ref[:, 0:D_MODEL].astype(F32))
    gb = _sigmoid(gate_ref[:, D_MODEL:].astype(F32))
    merged = (ga * ma + gb * mb).astype(BF16)
    y = jnp.dot(merged, wo_ref[...], preferred_element_type=F32)
    o_ref[...] = x_ref[...] + _rms_rows(y, g_ref[...], NORM_EPS)


def _pad_heads(w, n_heads):
    d = w.shape[0]
    w = w.reshape(d, n_heads, FOX_HEAD_DIM)
    w = jnp.pad(w, ((0, 0), (0, 0), (0, LANES - FOX_HEAD_DIM)))
    return w.reshape(d, n_heads * LANES)


def _placement():
    src = jnp.arange(N_SPLIT * LANES)
    term, head = src // LANES, src % LANES
    dst = head * LANES + AUG0 + term
    valid = head < FOX_HEADS
    cols = jnp.arange(FOX_HEADS * LANES)
    return ((cols[None, :] == dst[:, None]) & valid[:, None]).astype(BF16)


def _rope_tables(seq):
    pos = jnp.arange(seq, dtype=F32)
    inv_freq = ROPE_THETA ** (-jnp.arange(0, DIFF_QK_DIM, 2, dtype=F32) / DIFF_QK_DIM)
    ang = pos[:, None] * inv_freq[None, :]
    cos, sin = jnp.cos(ang), jnp.sin(ang)
    reps = LANES // (DIFF_QK_DIM // 2)
    cos_t = jnp.tile(cos, (1, reps))
    sin_t = jnp.tile(jnp.concatenate([-sin, sin], axis=1), (1, reps // 2))
    return cos_t, sin_t


def kernel(x, g_pre, w_in, b_forget, lambda_q1, lambda_k1, lambda_q2, lambda_k2,
           g_subln, w_branch, w_out, g_post):
    batch, seq, d = x.shape
    assert d == D_MODEL and seq % TM == 0 and TM == TQ == TK
    n_rows = batch * seq
    n_tiles = n_rows // TM
    tiles_per_seq = seq // TM

    w = w_in[0]
    sizes = [WIDTH, WIDTH, WIDTH, FOX_HEADS, WIDTH, WIDTH, WIDTH, WIDTH, WIDTH, 2 * D_MODEL]
    offs = [0]
    for s in sizes:
        offs.append(offs[-1] + s)
    w_qa, w_ka, w_va, w_fa, w_za, w_qb, w_kb, w_vb, w_zb, w_gate = (
        w[:, offs[n]:offs[n + 1]] for n in range(len(sizes)))
    wq_f = _pad_heads(w_qa, FOX_HEADS).astype(BF16)
    wk_f = _pad_heads(w_ka, FOX_HEADS).astype(BF16)
    wf = jnp.pad(w_fa, ((0, 0), (0, LANES - FOX_HEADS))).astype(BF16)
    bf = jnp.pad(b_forget[0], (0, LANES - FOX_HEADS)).reshape(1, LANES).astype(F32)
    w_nat = jnp.concatenate([w_qb, w_kb, w_zb, w_gate], axis=1).astype(BF16)
    w_tr = jnp.concatenate([w_va, w_za, w_vb], axis=1).T.astype(BF16)
    g_pre2 = g_pre[0].reshape(1, D_MODEL)
    g_post2 = g_post[0].reshape(1, D_MODEL)
    g_sub2 = g_subln[0].reshape(DIFF_V_DIM, 1)
    lamv = jnp.pad(jnp.stack([lambda_q1[0], lambda_k1[0], lambda_q2[0], lambda_k2[0]]),
                   ((0, 0), (0, LANES - DIFF_QK_DIM))).astype(F32)
    wa = w_branch[0, 0].astype(BF16)
    wb = w_branch[0, 1].astype(BF16)
    wo = w_out[0].astype(BF16)
    cos_t, sin_t = _rope_tables(seq)
    x2 = x.reshape(n_rows, D_MODEL)

    x_spec = pl.BlockSpec((TM, D_MODEL), lambda i: (i, 0))
    g_spec = _const_spec((1, D_MODEL))

    qk_shape = jax.ShapeDtypeStruct((batch, FOX_HEADS, seq, LANES), BF16)
    qk_spec = pl.BlockSpec((None, FOX_HEADS, TM, LANES),
                           lambda i: (i // tiles_per_seq, 0, i % tiles_per_seq, 0))
    q_fox, k_fox = pl.pallas_call(
        functools.partial(_fox_qk_kernel, tiles_per_seq),
        out_shape=(qk_shape, qk_shape),
        grid=(n_tiles,),
        in_specs=[x_spec, g_spec,
                  _const_spec(wq_f.shape), _const_spec(wk_f.shape),
                  _const_spec(wf.shape), _const_spec(bf.shape),
                  _const_spec((N_SPLIT * LANES, FOX_HEADS * LANES))],
        out_specs=(qk_spec, qk_spec),
        scratch_shapes=[pltpu.VMEM((1, LANES), F32)],
        compiler_params=pltpu.CompilerParams(
            dimension_semantics=("arbitrary",), vmem_limit_bytes=VMEM_LIMIT),
        name="fox_qk_proj",
    )(x2, g_pre2, wq_f, wk_f, wf, bf, _placement())

    row_spec = lambda n: pl.BlockSpec((TM, n), lambda i: (i, 0))
    rope_spec = pl.BlockSpec((TM, LANES), lambda i: (i % tiles_per_seq, 0))
    q_diff, k_diff, z_diff, gates = pl.pallas_call(
        _nat_kernel,
        out_shape=(jax.ShapeDtypeStruct((n_rows, WIDTH), BF16),
                   jax.ShapeDtypeStruct((n_rows, WIDTH), BF16),
                   jax.ShapeDtypeStruct((n_rows, WIDTH), BF16),
                   jax.ShapeDtypeStruct((n_rows, 2 * D_MODEL), BF16)),
        grid=(n_tiles,),
        in_specs=[x_spec, g_spec, _const_spec(w_nat.shape), rope_spec, rope_spec],
        out_specs=(row_spec(WIDTH), row_spec(WIDTH), row_spec(WIDTH),
                   row_spec(2 * D_MODEL)),
        compiler_params=pltpu.CompilerParams(
            dimension_semantics=("parallel",), vmem_limit_bytes=VMEM_LIMIT),
        name="nat_proj",
    )(x2, g_pre2, w_nat, cos_t, sin_t)

    tr_shape = jax.ShapeDtypeStruct((n_tiles, WIDTH, TM), BF16)
    tr_spec = pl.BlockSpec((None, WIDTH, TM), lambda i: (i, 0, 0))
    v_fox_t, z_fox_t, v_diff_t = pl.pallas_call(
        _tr_kernel,
        out_shape=(tr_shape, tr_shape, tr_shape),
        grid=(n_tiles,),
        in_specs=[x_spec, g_spec, _const_spec(w_tr.shape)],
        out_specs=(tr_spec, tr_spec, tr_spec),
        compiler_params=pltpu.CompilerParams(
            dimension_semantics=("parallel",), vmem_limit_bytes=VMEM_LIMIT),
        name="tr_proj",
    )(x2, g_pre2, w_tr)
    v_fox_t = v_fox_t.reshape(batch, tiles_per_seq, WIDTH, TM)
    z_fox_t = z_fox_t.reshape(batch, tiles_per_seq, WIDTH, TM)
    v_diff_t = v_diff_t.reshape(batch, tiles_per_seq, WIDTH, TM)

    ya_t = pl.pallas_call(
        _fox_attn_kernel,
        out_shape=jax.ShapeDtypeStruct((batch, tiles_per_seq, WIDTH, TM), BF16),
        grid=(batch, FOX_HEADS, tiles_per_seq),
        in_specs=[
            pl.BlockSpec((None, None, TQ, LANES), lambda b, h, i: (b, h, i, 0)),
            pl.BlockSpec((None, None, seq, LANES), lambda b, h, i: (b, h, 0, 0)),
            pl.BlockSpec((None, tiles_per_seq, FOX_HEAD_DIM, TK),
                         lambda b, h, i: (b, 0, h, 0)),
            pl.BlockSpec((None, None, FOX_HEAD_DIM, TQ), lambda b, h, i: (b, i, h, 0)),
        ],
        out_specs=pl.BlockSpec((None, None, FOX_HEAD_DIM, TQ),
                               lambda b, h, i: (b, i, h, 0)),
        scratch_shapes=[pltpu.VMEM((FOX_HEAD_DIM, TQ), F32),
                        pltpu.VMEM((1, TQ), F32), pltpu.VMEM((1, TQ), F32)],
        compiler_params=pltpu.CompilerParams(
            dimension_semantics=("parallel", "parallel", "arbitrary"),
            vmem_limit_bytes=VMEM_LIMIT),
        name="fox_attn",
    )(q_fox, k_fox, v_fox_t, z_fox_t)

    q_diff3 = q_diff.reshape(batch, seq, WIDTH)
    k_diff3 = k_diff.reshape(batch, seq, WIDTH)
    z_diff3 = z_diff.reshape(batch, seq, WIDTH)
    yb = pl.pallas_call(
        _diff_attn_kernel,
        out_shape=jax.ShapeDtypeStruct((batch, seq, WIDTH), BF16),
        grid=(batch, DIFF_HEADS, tiles_per_seq),
        in_specs=[
            pl.BlockSpec((4, LANES), lambda b, h, i: (0, 0)),
            pl.BlockSpec((None, TQ, LANES), lambda b, h, i: (b, i, h)),
            pl.BlockSpec((None, seq, LANES), lambda b, h, i: (b, 0, h)),
            pl.BlockSpec((None, tiles_per_seq, DIFF_V_DIM, TK),
                         lambda b, h, i: (b, 0, h, 0)),
            pl.BlockSpec((None, TQ, LANES), lambda b, h, i: (b, i, h)),
            pl.BlockSpec((DIFF_V_DIM, 1), lambda b, h, i: (0, 0)),
        ],
        out_specs=pl.BlockSpec((None, TQ, LANES), lambda b, h, i: (b, i, h)),
        scratch_shapes=[pltpu.VMEM((DIFF_V_DIM, TQ), F32), pltpu.VMEM((DIFF_V_DIM, TQ), F32),
                        pltpu.VMEM((1, TQ), F32), pltpu.VMEM((1, TQ), F32),
                        pltpu.VMEM((1, TQ), F32), pltpu.VMEM((1, TQ), F32)],
        compiler_params=pltpu.CompilerParams(
            dimension_semantics=("parallel", "parallel", "arbitrary"),
            vmem_limit_bytes=VMEM_LIMIT),
        name="diff_attn",
    )(lamv, q_diff3, k_diff3, v_diff_t, z_diff3, g_sub2)

    out = pl.pallas_call(
        _merge_kernel,
        out_shape=jax.ShapeDtypeStruct((n_rows, D_MODEL), F32),
        grid=(n_tiles,),
        in_specs=[
            pl.BlockSpec((None, WIDTH, TM), lambda i: (i, 0, 0)),
            row_spec(WIDTH), row_spec(2 * D_MODEL), x_spec,
            _const_spec(wa.shape), _const_spec(wb.shape), _const_spec(wo.shape),
            g_spec,
        ],
        out_specs=x_spec,
        compiler_params=pltpu.CompilerParams(
            dimension_semantics=("parallel",), vmem_limit_bytes=VMEM_LIMIT),
        name="merge_out",
    )(ya_t.reshape(n_tiles, WIDTH, TM), yb.reshape(n_rows, WIDTH), gates, x2,
      wa, wb, wo, g_post2)
    return out.reshape(batch, seq, D_MODEL)
```

```python
import functools
import math

import jax
import jax.numpy as jnp
from jax import lax
from jax.experimental import pallas as pl
from jax.experimental.pallas import tpu as pltpu

F32 = jnp.float32
BF16 = jnp.bfloat16

D_MODEL = 1024
FOX_HEADS = 8
FOX_HEAD_DIM = 64
DIFF_HEADS = 4
DIFF_QK_DIM = 64
DIFF_V_DIM = 128
WIDTH = 512
ROPE_THETA = 10000.0
NORM_EPS = 1e-6
SUBLN_EPS = 1e-5
LAMBDA_INIT = 0.8 - 0.6 * math.exp(-0.3 * 0)

LANES = 128
TM = 512
TQ = 512
TK = 512
ONES_ROWS = 16
N_SPLIT = 3
AUG0 = FOX_HEAD_DIM
LOG2E = math.log2(math.e)
QK_SCALE = LOG2E / math.sqrt(FOX_HEAD_DIM)
NEG_BIG = -1e30
VMEM_LIMIT = 48 * 1024 * 1024

_NT = (((1,), (1,)), ((), ()))
_TN = (((0,), (0,)), ((), ()))


def _const_spec(shape):
    return pl.BlockSpec(shape, lambda *_: (0,) * len(shape),
                        pipeline_mode=pl.Buffered(1))


def _rms_rows(x, g, eps):
    ms = jnp.mean(x * x, axis=-1, keepdims=True)
    return x * lax.rsqrt(ms + eps) * g


def _split_bf16(x):
    parts = []
    r = x
    for _ in range(N_SPLIT - 1):
        p = r.astype(BF16)
        parts.append(p)
        r = r - p.astype(F32)
    parts.append(r.astype(BF16))
    return parts


def _log_sigmoid(x):
    return jnp.minimum(x, 0.0) - jnp.log(1.0 + jnp.exp(-jnp.abs(x)))


def _silu(z):
    return z * (1.0 / (1.0 + jnp.exp(-z)))


def _sigmoid(z):
    return 1.0 / (1.0 + jnp.exp(-z))


def _fox_qk_kernel(tiles_per_seq, x_ref, g_ref, wq_ref, wk_ref, wf_ref, bf_ref,
                   place_ref, q_ref, k_ref, carry_ref):
    i = pl.program_id(0)

    @pl.when(i % tiles_per_seq == 0)
    def _():
        carry_ref[...] = jnp.zeros_like(carry_ref)

    h = _rms_rows(x_ref[...], g_ref[...], NORM_EPS).astype(BF16)

    q = jnp.dot(h, wq_ref[...], preferred_element_type=F32)
    lane = lax.broadcasted_iota(jnp.int32, (1, q.shape[1]), 1) % LANES
    ones_lane = (lane >= AUG0) & (lane < AUG0 + N_SPLIT)
    q = jnp.where(ones_lane, 1.0, q * QK_SCALE).astype(BF16)
    for hh in range(FOX_HEADS):
        q_ref[hh] = q[:, hh * LANES:(hh + 1) * LANES]

    f = jnp.dot(h, wf_ref[...], preferred_element_type=F32) + bf_ref[...]
    logf = _log_sigmoid(f)
    rows = lax.broadcasted_iota(jnp.int32, (TM, TM), 0)
    cols = lax.broadcasted_iota(jnp.int32, (TM, TM), 1)
    tri = (rows >= cols).astype(BF16)
    c = carry_ref[...]
    for part in _split_bf16(logf):
        c = c + jnp.dot(tri, part, preferred_element_type=F32)
    carry_ref[...] = c[TM - 1:TM, :]

    neg_c = jnp.concatenate(_split_bf16(-LOG2E * c), axis=1)
    k = (jnp.dot(h, wk_ref[...], preferred_element_type=F32)
         + jnp.dot(neg_c, place_ref[...], preferred_element_type=F32)).astype(BF16)
    for hh in range(FOX_HEADS):
        k_ref[hh] = k[:, hh * LANES:(hh + 1) * LANES]


def _rope(x, cos, sin_signed):
    lane = lax.broadcasted_iota(jnp.int32, (1, LANES), 1) % DIFF_QK_DIM
    first_half = lane < DIFF_QK_DIM // 2
    outs = []
    for g in range(x.shape[1] // LANES):
        xg = x[:, g * LANES:(g + 1) * LANES]
        partner = jnp.where(first_half,
                            pltpu.roll(xg, LANES - DIFF_QK_DIM // 2, axis=1),
                            pltpu.roll(xg, DIFF_QK_DIM // 2, axis=1))
        outs.append(xg * cos + partner * sin_signed)
    return jnp.concatenate(outs, axis=1)


def _nat_kernel(x_ref, g_ref, w_ref, cos_ref, sin_ref, q_ref, k_ref, z_ref, gate_ref):
    h = _rms_rows(x_ref[...], g_ref[...], NORM_EPS).astype(BF16)
    cos = cos_ref[...]
    sin = sin_ref[...]
    q = jnp.dot(h, w_ref[:, 0:WIDTH], preferred_element_type=F32)
    q_ref[...] = (_rope(q, cos, sin) * QK_SCALE).astype(BF16)
    k = jnp.dot(h, w_ref[:, WIDTH:2 * WIDTH], preferred_element_type=F32)
    k_ref[...] = _rope(k, cos, sin).astype(BF16)
    z_ref[...] = jnp.dot(h, w_ref[:, 2 * WIDTH:3 * WIDTH],
                         preferred_element_type=F32).astype(BF16)
    gate_ref[...] = jnp.dot(h, w_ref[:, 3 * WIDTH:],
                            preferred_element_type=F32).astype(BF16)


def _tr_kernel(x_ref, g_ref, wt_ref, vf_ref, zf_ref, vd_ref):
    h = _rms_rows(x_ref[...], g_ref[...], NORM_EPS).astype(BF16)
    for idx, ref in enumerate((vf_ref, zf_ref, vd_ref)):
        w = wt_ref[idx * WIDTH:(idx + 1) * WIDTH, :]
        ref[...] = lax.dot_general(w, h, _NT,
                                   preferred_element_type=F32).astype(BF16)


def _causal_mask(s_t):
    kpos = lax.broadcasted_iota(jnp.int32, s_t.shape, 0)
    qpos = lax.broadcasted_iota(jnp.int32, s_t.shape, 1)
    return jnp.where(kpos <= qpos, s_t, NEG_BIG)


def _two_chain_attention(i, qk_fn, vt_fn, rows, acc_ref, m_ref, al_ref, s_ref, p1_ref):
    aug = rows + ONES_ROWS
    ones = jnp.ones((ONES_ROWS, TK), BF16)

    def softmax(c, s_t, masked):
        if masked:
            s_t = _causal_mask(s_t)
        m_old = m_ref[c:c + 1, :]
        m_new = jnp.maximum(m_old, jnp.max(s_t, axis=0, keepdims=True))
        m_ref[c:c + 1, :] = m_new
        return jnp.exp2(s_t - m_new).astype(BF16), jnp.exp2(m_old - m_new)

    def accumulate(c, j, alpha, p):
        vt = jnp.concatenate([vt_fn(c, j), ones], axis=0)
        sl = slice(c * aug, (c + 1) * aug)
        acc_ref[sl, :] = alpha * acc_ref[sl, :] + jnp.dot(
            vt, p, preferred_element_type=F32)

    def iteration(j, masked, slot):
        accumulate(1, jnp.maximum(j - 1, 0), al_ref[...], p1_ref[...])
        if not masked:
            for c in range(2):
                s_ref[1 - slot, c] = qk_fn(c, j + 1)
        p0, alpha0 = softmax(0, s_ref[slot, 0], masked)
        accumulate(0, j, alpha0, p0)
        p1, alpha1 = softmax(1, s_ref[slot, 1], masked)
        p1_ref[...] = p1
        al_ref[...] = alpha1

    m_ref[...] = jnp.full_like(m_ref, NEG_BIG)
    acc_ref[...] = jnp.zeros_like(acc_ref)
    p1_ref[...] = jnp.zeros_like(p1_ref)
    al_ref[...] = jnp.ones_like(al_ref)
    for c in range(2):
        s_ref[0, c] = qk_fn(c, 0)

    def pair(t, carry):
        iteration(2 * t, False, 0)
        iteration(2 * t + 1, False, 1)
        return carry

    lax.fori_loop(0, i // 2, pair, 0)
    odd = i % 2 == 1

    @pl.when(odd)
    def _():
        iteration(i - 1, False, 0)
        iteration(i, True, 1)
        accumulate(1, i, al_ref[...], p1_ref[...])

    @pl.when(jnp.logical_not(odd))
    def _():
        iteration(i, True, 0)
        accumulate(1, i, al_ref[...], p1_ref[...])


def _kv_rows(j):
    return pl.ds(pl.multiple_of(j * TK, TK), TK)


def _normalized(acc_ref, c, rows):
    base = c * (rows + ONES_ROWS)
    return acc_ref[base:base + rows, :] * (1.0 / acc_ref[base + rows:base + rows + 1, :])


def _fox_attn_kernel(q_ref, k_ref, vt_ref, zt_ref, o_ref, acc_ref, m_ref, al_ref,
                     s_ref, p1_ref):
    d = FOX_HEAD_DIM

    def qk_fn(c, j):
        return lax.dot_general(k_ref[c, _kv_rows(j), :], q_ref[c], _NT,
                               preferred_element_type=F32)

    def vt_fn(c, j):
        return vt_ref[j, c * d:(c + 1) * d, :]

    _two_chain_attention(pl.program_id(2), qk_fn, vt_fn, d, acc_ref, m_ref, al_ref,
                         s_ref, p1_ref)
    for c in range(2):
        sl = slice(c * d, (c + 1) * d)
        z = zt_ref[sl, :].astype(F32)
        o_ref[sl, :] = (_normalized(acc_ref, c, d) * _silu(z)).astype(BF16)


def _diff_attn_kernel(lamv_ref, q_ref, k_ref, vt_ref, z_ref, g_ref, o_ref,
                      acc_ref, m_ref, al_ref, s_ref, p1_ref, qm_ref):
    q = q_ref[...]
    lane = lax.broadcasted_iota(jnp.int32, (1, LANES), 1)
    zero = jnp.zeros_like(q)
    qm_ref[0] = jnp.where(lane < DIFF_QK_DIM, q, zero)
    qm_ref[1] = jnp.where(lane >= DIFF_QK_DIM, q, zero)

    def qk_fn(c, j):
        return lax.dot_general(k_ref[_kv_rows(j), :], qm_ref[c], _NT,
                               preferred_element_type=F32)

    def vt_fn(c, j):
        return vt_ref[j]

    _two_chain_attention(pl.program_id(2), qk_fn, vt_fn, DIFF_V_DIM, acc_ref, m_ref,
                         al_ref, s_ref, p1_ref)

    lamv = lamv_ref[...]
    lam = (jnp.exp(jnp.sum(lamv[0:1] * lamv[1:2], axis=1, keepdims=True))
           - jnp.exp(jnp.sum(lamv[2:3] * lamv[3:4], axis=1, keepdims=True))
           + LAMBDA_INIT)
    o = (_normalized(acc_ref, 0, DIFF_V_DIM)
         - lam * _normalized(acc_ref, 1, DIFF_V_DIM))
    ms = jnp.mean(o * o, axis=0, keepdims=True)
    y = o * lax.rsqrt(ms + SUBLN_EPS) * g_ref[...] * (1.0 - LAMBDA_INIT)
    z = z_ref[...].astype(F32)
    o_ref[...] = (y.T * _silu(z)).astype(BF16)


def _merge_kernel(yat_ref, yb_ref, gate_ref, x_ref, wa_ref, wb_ref, wo_ref, g_ref, o_ref):
    ma = lax.dot_general(yat_ref[...], wa_ref[...], _TN, preferred_element_type=F32)
    mb = jnp.dot(yb_ref[...], wb_ref[...], preferred_element_type=F32)
    ga = _sigmoid(gate_ref[:, 0:D_MODEL].astype(F32))
    gb = _sigmoid(gate_ref[:, D_MODEL:].astype(F32))
    merged = (ga * ma + gb * mb).astype(BF16)
    y = jnp.dot(merged, wo_ref[...], preferred_element_type=F32)
    o_ref[...] = x_ref[...] + _rms_rows(y, g_ref[...], NORM_EPS)


_ATTN_PARAMS = pltpu.CompilerParams(
    dimension_semantics=("parallel", "parallel", "arbitrary"),
    vmem_limit_bytes=VMEM_LIMIT)


def _attn_scratch(rows):
    return [pltpu.VMEM((2 * (rows + ONES_ROWS), TQ), F32), pltpu.VMEM((2, TQ), F32),
            pltpu.VMEM((1, TQ), F32), pltpu.VMEM((2, 2, TK, TQ), F32),
            pltpu.VMEM((TK, TQ), BF16)]


def _pad_heads(w, n_heads):
    d = w.shape[0]
    w = w.reshape(d, n_heads, FOX_HEAD_DIM)
    w = jnp.pad(w, ((0, 0), (0, 0), (0, LANES - FOX_HEAD_DIM)))
    return w.reshape(d, n_heads * LANES)


def _placement():
    src = jnp.arange(N_SPLIT * LANES)
    term, head = src // LANES, src % LANES
    dst = head * LANES + AUG0 + term
    valid = head < FOX_HEADS
    cols = jnp.arange(FOX_HEADS * LANES)
    return ((cols[None, :] == dst[:, None]) & valid[:, None]).astype(BF16)


def _rope_tables(seq):
    pos = jnp.arange(seq, dtype=F32)
    inv_freq = ROPE_THETA ** (-jnp.arange(0, DIFF_QK_DIM, 2, dtype=F32) / DIFF_QK_DIM)
    ang = pos[:, None] * inv_freq[None, :]
    cos, sin = jnp.cos(ang), jnp.sin(ang)
    reps = LANES // (DIFF_QK_DIM // 2)
    cos_t = jnp.tile(cos, (1, reps))
    sin_t = jnp.tile(jnp.concatenate([-sin, sin], axis=1), (1, reps // 2))
    return cos_t, sin_t


def kernel(x, g_pre, w_in, b_forget, lambda_q1, lambda_k1, lambda_q2, lambda_k2,
           g_subln, w_branch, w_out, g_post):
    batch, seq, d = x.shape
    assert d == D_MODEL and seq % TM == 0 and TM == TQ == TK
    n_rows = batch * seq
    n_tiles = n_rows // TM
    tiles_per_seq = seq // TM

    w = w_in[0]
    sizes = [WIDTH, WIDTH, WIDTH, FOX_HEADS, WIDTH, WIDTH, WIDTH, WIDTH, WIDTH, 2 * D_MODEL]
    offs = [0]
    for s in sizes:
        offs.append(offs[-1] + s)
    w_qa, w_ka, w_va, w_fa, w_za, w_qb, w_kb, w_vb, w_zb, w_gate = (
        w[:, offs[n]:offs[n + 1]] for n in range(len(sizes)))
    wq_f = _pad_heads(w_qa, FOX_HEADS).astype(BF16)
    wk_f = _pad_heads(w_ka, FOX_HEADS).astype(BF16)
    wf = jnp.pad(w_fa, ((0, 0), (0, LANES - FOX_HEADS))).astype(BF16)
    bf = jnp.pad(b_forget[0], (0, LANES - FOX_HEADS)).reshape(1, LANES).astype(F32)
    w_nat = jnp.concatenate([w_qb, w_kb, w_zb, w_gate], axis=1).astype(BF16)
    w_tr = jnp.concatenate([w_va, w_za, w_vb], axis=1).T.astype(BF16)
    g_pre2 = g_pre[0].reshape(1, D_MODEL)
    g_post2 = g_post[0].reshape(1, D_MODEL)
    g_sub2 = g_subln[0].reshape(DIFF_V_DIM, 1)
    lamv = jnp.pad(jnp.stack([lambda_q1[0], lambda_k1[0], lambda_q2[0], lambda_k2[0]]),
                   ((0, 0), (0, LANES - DIFF_QK_DIM))).astype(F32)
    wa = w_branch[0, 0].astype(BF16)
    wb = w_branch[0, 1].astype(BF16)
    wo = w_out[0].astype(BF16)
    cos_t, sin_t = _rope_tables(seq)
    x2 = x.reshape(n_rows, D_MODEL)

    x_spec = pl.BlockSpec((TM, D_MODEL), lambda i: (i, 0))
    g_spec = _const_spec((1, D_MODEL))

    qk_shape = jax.ShapeDtypeStruct((batch, FOX_HEADS, seq, LANES), BF16)
    qk_spec = pl.BlockSpec((None, FOX_HEADS, TM, LANES),
                           lambda i: (i // tiles_per_seq, 0, i % tiles_per_seq, 0))
    q_fox, k_fox = pl.pallas_call(
        functools.partial(_fox_qk_kernel, tiles_per_seq),
        out_shape=(qk_shape, qk_shape),
        grid=(n_tiles,),
        in_specs=[x_spec, g_spec,
                  _const_spec(wq_f.shape), _const_spec(wk_f.shape),
                  _const_spec(wf.shape), _const_spec(bf.shape),
                  _const_spec((N_SPLIT * LANES, FOX_HEADS * LANES))],
        out_specs=(qk_spec, qk_spec),
        scratch_shapes=[pltpu.VMEM((1, LANES), F32)],
        compiler_params=pltpu.CompilerParams(
            dimension_semantics=("arbitrary",), vmem_limit_bytes=VMEM_LIMIT),
        name="fox_qk_proj",
    )(x2, g_pre2, wq_f, wk_f, wf, bf, _placement())

    row_spec = lambda n: pl.BlockSpec((TM, n), lambda i: (i, 0))
    rope_spec = pl.BlockSpec((TM, LANES), lambda i: (i % tiles_per_seq, 0))
    q_diff, k_diff, z_diff, gates = pl.pallas_call(
        _nat_kernel,
        out_shape=(jax.ShapeDtypeStruct((n_rows, WIDTH), BF16),
                   jax.ShapeDtypeStruct((n_rows, WIDTH), BF16),
                   jax.ShapeDtypeStruct((n_rows, WIDTH), BF16),
                   jax.ShapeDtypeStruct((n_rows, 2 * D_MODEL), BF16)),
        grid=(n_tiles,),
        in_specs=[x_spec, g_spec, _const_spec(w_nat.shape), rope_spec, rope_spec],
        out_specs=(row_spec(WIDTH), row_spec(WIDTH), row_spec(WIDTH),
                   row_spec(2 * D_MODEL)),
        compiler_params=pltpu.CompilerParams(
            dimension_semantics=("parallel",), vmem_limit_bytes=VMEM_LIMIT),
        name="nat_proj",
    )(x2, g_pre2, w_nat, cos_t, sin_t)

    tr_shape = jax.ShapeDtypeStruct((n_tiles, WIDTH, TM), BF16)
    tr_spec = pl.BlockSpec((None, WIDTH, TM), lambda i: (i, 0, 0))
    v_fox_t, z_fox_t, v_diff_t = pl.pallas_call(
        _tr_kernel,
        out_shape=(tr_shape, tr_shape, tr_shape),
        grid=(n_tiles,),
        in_specs=[x_spec, g_spec, _const_spec(w_tr.shape)],
        out_specs=(tr_spec, tr_spec, tr_spec),
        compiler_params=pltpu.CompilerParams(
            dimension_semantics=("parallel",), vmem_limit_bytes=VMEM_LIMIT),
        name="tr_proj",
    )(x2, g_pre2, w_tr)
    v_fox_t = v_fox_t.reshape(batch, tiles_per_seq, WIDTH, TM)
    z_fox_t = z_fox_t.reshape(batch, tiles_per_seq, WIDTH, TM)
    v_diff_t = v_diff_t.reshape(batch, tiles_per_seq, WIDTH, TM)

    ya_t = pl.pallas_call(
        _fox_attn_kernel,
        out_shape=jax.ShapeDtypeStruct((batch, tiles_per_seq, WIDTH, TM), BF16),
        grid=(batch, FOX_HEADS // 2, tiles_per_seq),
        in_specs=[
            pl.BlockSpec((None, 2, TQ, LANES), lambda b, h, i: (b, h, i, 0)),
            pl.BlockSpec((None, 2, seq, LANES), lambda b, h, i: (b, h, 0, 0)),
            pl.BlockSpec((None, tiles_per_seq, 2 * FOX_HEAD_DIM, TK),
                         lambda b, h, i: (b, 0, h, 0)),
            pl.BlockSpec((None, None, 2 * FOX_HEAD_DIM, TQ), lambda b, h, i: (b, i, h, 0)),
        ],
        out_specs=pl.BlockSpec((None, None, 2 * FOX_HEAD_DIM, TQ),
                               lambda b, h, i: (b, i, h, 0)),
        scratch_shapes=_attn_scratch(FOX_HEAD_DIM),
        compiler_params=_ATTN_PARAMS,
        name="fox_attn",
    )(q_fox, k_fox, v_fox_t, z_fox_t)

    q_diff3 = q_diff.reshape(batch, seq, WIDTH)
    k_diff3 = k_diff.reshape(batch, seq, WIDTH)
    z_diff3 = z_diff.reshape(batch, seq, WIDTH)
    yb = pl.pallas_call(
        _diff_attn_kernel,
        out_shape=jax.ShapeDtypeStruct((batch, seq, WIDTH), BF16),
        grid=(batch, DIFF_HEADS, tiles_per_seq),
        in_specs=[
            pl.BlockSpec((4, LANES), lambda b, h, i: (0, 0)),
            pl.BlockSpec((None, TQ, LANES), lambda b, h, i: (b, i, h)),
            pl.BlockSpec((None, seq, LANES), lambda b, h, i: (b, 0, h)),
            pl.BlockSpec((None, tiles_per_seq, DIFF_V_DIM, TK),
                         lambda b, h, i: (b, 0, h, 0)),
            pl.BlockSpec((None, TQ, LANES), lambda b, h, i: (b, i, h)),
            pl.BlockSpec((DIFF_V_DIM, 1), lambda b, h, i: (0, 0)),
        ],
        out_specs=pl.BlockSpec((None, TQ, LANES), lambda b, h, i: (b, i, h)),
        scratch_shapes=_attn_scratch(DIFF_V_DIM) + [pltpu.VMEM((2, TQ, LANES), BF16)],
        compiler_params=_ATTN_PARAMS,
        name="diff_attn",
    )(lamv, q_diff3, k_diff3, v_diff_t, z_diff3, g_sub2)

    out = pl.pallas_call(
        _merge_kernel,
        out_shape=jax.ShapeDtypeStruct((n_rows, D_MODEL), F32),
        grid=(n_tiles,),
        in_specs=[
            pl.BlockSpec((None, WIDTH, TM), lambda i: (i, 0, 0)),
            row_spec(WIDTH), row_spec(2 * D_MODEL), x_spec,
            _const_spec(wa.shape), _const_spec(wb.shape), _const_spec(wo.shape),
            g_spec,
        ],
        out_specs=x_spec,
        compiler_params=pltpu.CompilerParams(
            dimension_semantics=("parallel",), vmem_limit_bytes=VMEM_LIMIT),
        name="merge_out",
    )(ya_t.reshape(n_tiles, WIDTH, TM), yb.reshape(n_rows, WIDTH), gates, x2,
      wa, wb, wo, g_post2)
    return out.reshape(batch, seq, D_MODEL)
```

```python
import functools
import math

import jax
import jax.numpy as jnp
from jax import lax
from jax.experimental import pallas as pl
from jax.experimental.pallas import tpu as pltpu

F32 = jnp.float32
BF16 = jnp.bfloat16

D_MODEL = 1024
FOX_HEADS = 8
FOX_HEAD_DIM = 64
DIFF_HEADS = 4
DIFF_QK_DIM = 64
DIFF_V_DIM = 128
WIDTH = 512
ROPE_THETA = 10000.0
NORM_EPS = 1e-6
SUBLN_EPS = 1e-5
LAMBDA_INIT = 0.8 - 0.6 * math.exp(-0.3 * 0)

LANES = 128
TM = 512
TQ = 1024
TK = 512
ONES_ROWS = 16
N_SPLIT = 3
AUG0 = FOX_HEAD_DIM
LOG2E = math.log2(math.e)
QK_SCALE = LOG2E / math.sqrt(FOX_HEAD_DIM)
NEG_BIG = -1e30
VMEM_LIMIT = 48 * 1024 * 1024

_NT = (((1,), (1,)), ((), ()))
_TN = (((0,), (0,)), ((), ()))


def _const_spec(shape):
    return pl.BlockSpec(shape, lambda *_: (0,) * len(shape),
                        pipeline_mode=pl.Buffered(1))


def _rms_rows(x, g, eps):
    ms = jnp.mean(x * x, axis=-1, keepdims=True)
    return x * lax.rsqrt(ms + eps) * g


def _split_bf16(x):
    parts = []
    r = x
    for _ in range(N_SPLIT - 1):
        p = r.astype(BF16)
        parts.append(p)
        r = r - p.astype(F32)
    parts.append(r.astype(BF16))
    return parts


def _log_sigmoid(x):
    return jnp.minimum(x, 0.0) - jnp.log(1.0 + jnp.exp(-jnp.abs(x)))


def _silu(z):
    return z * (1.0 / (1.0 + jnp.exp(-z)))


def _sigmoid(z):
    return 1.0 / (1.0 + jnp.exp(-z))


def _fox_qk_kernel(tiles_per_seq, x_ref, g_ref, wq_ref, wk_ref, wf_ref, bf_ref,
                   place_ref, q_ref, k_ref, carry_ref):
    i = pl.program_id(0)

    @pl.when(i % tiles_per_seq == 0)
    def _():
        carry_ref[...] = jnp.zeros_like(carry_ref)

    h = _rms_rows(x_ref[...], g_ref[...], NORM_EPS).astype(BF16)

    q = jnp.dot(h, wq_ref[...], preferred_element_type=F32)
    lane = lax.broadcasted_iota(jnp.int32, (1, q.shape[1]), 1) % LANES
    ones_lane = (lane >= AUG0) & (lane < AUG0 + N_SPLIT)
    q = jnp.where(ones_lane, 1.0, q * QK_SCALE).astype(BF16)
    for hh in range(FOX_HEADS):
        q_ref[hh] = q[:, hh * LANES:(hh + 1) * LANES]

    f = jnp.dot(h, wf_ref[...], preferred_element_type=F32) + bf_ref[...]
    logf = _log_sigmoid(f)
    rows = lax.broadcasted_iota(jnp.int32, (TM, TM), 0)
    cols = lax.broadcasted_iota(jnp.int32, (TM, TM), 1)
    tri = (rows >= cols).astype(BF16)
    c = carry_ref[...]
    for part in _split_bf16(logf):
        c = c + jnp.dot(tri, part, preferred_element_type=F32)
    carry_ref[...] = c[TM - 1:TM, :]

    neg_c = jnp.concatenate(_split_bf16(-LOG2E * c), axis=1)
    k = (jnp.dot(h, wk_ref[...], preferred_element_type=F32)
         + jnp.dot(neg_c, place_ref[...], preferred_element_type=F32)).astype(BF16)
    for hh in range(FOX_HEADS):
        k_ref[hh] = k[:, hh * LANES:(hh + 1) * LANES]


def _rope(x, cos, sin_signed):
    lane = lax.broadcasted_iota(jnp.int32, (1, LANES), 1) % DIFF_QK_DIM
    first_half = lane < DIFF_QK_DIM // 2
    outs = []
    for g in range(x.shape[1] // LANES):
        xg = x[:, g * LANES:(g + 1) * LANES]
        partner = jnp.where(first_half,
                            pltpu.roll(xg, LANES - DIFF_QK_DIM // 2, axis=1),
                            pltpu.roll(xg, DIFF_QK_DIM // 2, axis=1))
        outs.append(xg * cos + partner * sin_signed)
    return jnp.concatenate(outs, axis=1)


def _nat_kernel(x_ref, g_ref, w_ref, cos_ref, sin_ref, q_ref, k_ref, z_ref, gate_ref):
    h = _rms_rows(x_ref[...], g_ref[...], NORM_EPS).astype(BF16)
    cos = cos_ref[...]
    sin = sin_ref[...]
    q = jnp.dot(h, w_ref[:, 0:WIDTH], preferred_element_type=F32)
    q_ref[...] = (_rope(q, cos, sin) * QK_SCALE).astype(BF16)
    k = jnp.dot(h, w_ref[:, WIDTH:2 * WIDTH], preferred_element_type=F32)
    k_ref[...] = _rope(k, cos, sin).astype(BF16)
    z_ref[...] = jnp.dot(h, w_ref[:, 2 * WIDTH:3 * WIDTH],
                         preferred_element_type=F32).astype(BF16)
    gate_ref[...] = jnp.dot(h, w_ref[:, 3 * WIDTH:],
                            preferred_element_type=F32).astype(BF16)


def _tr_kernel(x_ref, g_ref, wt_ref, vf_ref, zf_ref, vd_ref):
    h = _rms_rows(x_ref[...], g_ref[...], NORM_EPS).astype(BF16)
    for idx, ref in enumerate((vf_ref, zf_ref, vd_ref)):
        w = wt_ref[idx * WIDTH:(idx + 1) * WIDTH, :]
        ref[...] = lax.dot_general(w, h, _NT,
                                   preferred_element_type=F32).astype(BF16)


def _two_chain_attention(i, qk_fn, vt_fn, rows, acc_ref, m_ref, al_ref, s_ref, bm_ref,
                         p1_ref):
    aug = rows + ONES_ROWS
    ones = jnp.ones((ONES_ROWS, TK), BF16)

    def produce(c, j, slot, causal):
        s_t = qk_fn(c, j)
        if causal is not None:
            kpos = lax.broadcasted_iota(jnp.int32, s_t.shape, 0)
            qpos = lax.broadcasted_iota(jnp.int32, s_t.shape, 1)
            keep = kpos - qpos <= TQ * i - TK * j
            if causal is not True:
                keep = jnp.logical_or(keep, jnp.logical_not(causal))
            s_t = jnp.where(keep, s_t, NEG_BIG)
        s_ref[slot, c] = s_t
        bm_ref[slot, c:c + 1, :] = jnp.max(s_t, axis=0, keepdims=True)

    def consume(c, slot):
        m_old = m_ref[c:c + 1, :]
        m_new = jnp.maximum(m_old, bm_ref[slot, c:c + 1, :])
        m_ref[c:c + 1, :] = m_new
        return jnp.exp2(s_ref[slot, c] - m_new).astype(BF16), jnp.exp2(m_old - m_new)

    def accumulate(c, j, alpha, p):
        vt = jnp.concatenate([vt_fn(c, j), ones], axis=0)
        sl = slice(c * aug, (c + 1) * aug)
        acc_ref[sl, :] = alpha * acc_ref[sl, :] + jnp.dot(
            vt, p, preferred_element_type=F32)

    def iteration(j, slot, next_causal):
        if next_causal is not False:
            for c in range(2):
                produce(c, j + 1, 1 - slot, next_causal)
        accumulate(1, jnp.maximum(j - 1, 0), al_ref[...], p1_ref[...])
        p0, alpha0 = consume(0, slot)
        accumulate(0, j, alpha0, p0)
        p1, alpha1 = consume(1, slot)
        p1_ref[...] = p1
        al_ref[...] = alpha1

    m_ref[...] = jnp.full_like(m_ref, NEG_BIG)
    acc_ref[...] = jnp.zeros_like(acc_ref)
    p1_ref[...] = jnp.zeros_like(p1_ref)
    al_ref[...] = jnp.ones_like(al_ref)
    for c in range(2):
        produce(c, 0, 0, i == 0)

    last = 2 * i + 1

    def pair(t, carry):
        iteration(2 * t, 0, None)
        iteration(2 * t + 1, 1, None)
        return carry

    lax.fori_loop(0, i - 1, pair, 0)

    @pl.when(i > 0)
    def _():
        iteration(last - 3, 0, None)
        iteration(last - 2, 1, True)

    iteration(last - 1, 0, True)
    iteration(last, 1, False)
    accumulate(1, last, al_ref[...], p1_ref[...])


def _kv_rows(j):
    return pl.ds(pl.multiple_of(j * TK, TK), TK)


def _normalized(acc_ref, c, rows):
    base = c * (rows + ONES_ROWS)
    return acc_ref[base:base + rows, :] * (1.0 / acc_ref[base + rows:base + rows + 1, :])


def _fox_attn_kernel(q_ref, k_ref, vt_ref, zt_ref, o_ref, acc_ref, *scratch):
    d = FOX_HEAD_DIM

    def qk_fn(c, j):
        return lax.dot_general(k_ref[c, _kv_rows(j), :], q_ref[c], _NT,
                               preferred_element_type=F32)

    def vt_fn(c, j):
        return vt_ref[j, c * d:(c + 1) * d, :]

    _two_chain_attention(pl.program_id(2), qk_fn, vt_fn, d, acc_ref, *scratch)
    for c in range(2):
        sl = slice(c * d, (c + 1) * d)
        y = _normalized(acc_ref, c, d)
        for t in range(TQ // TM):
            z = zt_ref[t, sl, :].astype(F32)
            o_ref[t, sl, :] = (y[:, t * TM:(t + 1) * TM] * _silu(z)).astype(BF16)


def _diff_attn_kernel(lamv_ref, q_ref, k_ref, vt_ref, z_ref, g_ref, o_ref,
                      qm_ref, acc_ref, *scratch):
    q = q_ref[...]
    lane = lax.broadcasted_iota(jnp.int32, (1, LANES), 1)
    zero = jnp.zeros_like(q)
    qm_ref[0] = jnp.where(lane < DIFF_QK_DIM, q, zero)
    qm_ref[1] = jnp.where(lane >= DIFF_QK_DIM, q, zero)

    def qk_fn(c, j):
        return lax.dot_general(k_ref[_kv_rows(j), :], qm_ref[c], _NT,
                               preferred_element_type=F32)

    def vt_fn(c, j):
        return vt_ref[j]

    _two_chain_attention(pl.program_id(2), qk_fn, vt_fn, DIFF_V_DIM, acc_ref, *scratch)

    lamv = lamv_ref[...]
    lam = (jnp.exp(jnp.sum(lamv[0:1] * lamv[1:2], axis=1, keepdims=True))
           - jnp.exp(jnp.sum(lamv[2:3] * lamv[3:4], axis=1, keepdims=True))
           + LAMBDA_INIT)
    o = (_normalized(acc_ref, 0, DIFF_V_DIM)
         - lam * _normalized(acc_ref, 1, DIFF_V_DIM))
    ms = jnp.mean(o * o, axis=0, keepdims=True)
    y = o * lax.rsqrt(ms + SUBLN_EPS) * g_ref[...] * (1.0 - LAMBDA_INIT)
    z = z_ref[...].astype(F32)
    o_ref[...] = (y.T * _silu(z)).astype(BF16)


def _merge_kernel(yat_ref, yb_ref, gate_ref, x_ref, wa_ref, wb_ref, wo_ref, g_ref, o_ref):
    ma = lax.dot_general(yat_ref[...], wa_ref[...], _TN, preferred_element_type=F32)
    mb = jnp.dot(yb_ref[...], wb_ref[...], preferred_element_type=F32)
    ga = _sigmoid(gate_ref[:, 0:D_MODEL].astype(F32))
    gb = _sigmoid(gate_ref[:, D_MODEL:].astype(F32))
    merged = (ga * ma + gb * mb).astype(BF16)
    y = jnp.dot(merged, wo_ref[...], preferred_element_type=F32)
    o_ref[...] = x_ref[...] + _rms_rows(y, g_ref[...], NORM_EPS)


_ATTN_PARAMS = pltpu.CompilerParams(
    dimension_semantics=("parallel", "parallel", "arbitrary"),
    vmem_limit_bytes=VMEM_LIMIT)


def _attn_scratch(rows):
    return [pltpu.VMEM((2 * (rows + ONES_ROWS), TQ), F32), pltpu.VMEM((2, TQ), F32),
            pltpu.VMEM((1, TQ), F32), pltpu.VMEM((2, 2, TK, TQ), F32),
            pltpu.VMEM((2, 2, TQ), F32), pltpu.VMEM((TK, TQ), BF16)]


def _pad_heads(w, n_heads):
    d = w.shape[0]
    w = w.reshape(d, n_heads, FOX_HEAD_DIM)
    w = jnp.pad(w, ((0, 0), (0, 0), (0, LANES - FOX_HEAD_DIM)))
    return w.reshape(d, n_heads * LANES)


def _placement():
    src = jnp.arange(N_SPLIT * LANES)
    term, head = src // LANES, src % LANES
    dst = head * LANES + AUG0 + term
    valid = head < FOX_HEADS
    cols = jnp.arange(FOX_HEADS * LANES)
    return ((cols[None, :] == dst[:, None]) & valid[:, None]).astype(BF16)


def _rope_tables(seq):
    pos = jnp.arange(seq, dtype=F32)
    inv_freq = ROPE_THETA ** (-jnp.arange(0, DIFF_QK_DIM, 2, dtype=F32) / DIFF_QK_DIM)
    ang = pos[:, None] * inv_freq[None, :]
    cos, sin = jnp.cos(ang), jnp.sin(ang)
    reps = LANES // (DIFF_QK_DIM // 2)
    cos_t = jnp.tile(cos, (1, reps))
    sin_t = jnp.tile(jnp.concatenate([-sin, sin], axis=1), (1, reps // 2))
    return cos_t, sin_t


def kernel(x, g_pre, w_in, b_forget, lambda_q1, lambda_k1, lambda_q2, lambda_k2,
           g_subln, w_branch, w_out, g_post):
    batch, seq, d = x.shape
    assert d == D_MODEL and seq % TQ == 0 and TM == TK and TQ == 2 * TK
    n_rows = batch * seq
    n_tiles = n_rows // TM
    tiles_per_seq = seq // TM

    w = w_in[0]
    sizes = [WIDTH, WIDTH, WIDTH, FOX_HEADS, WIDTH, WIDTH, WIDTH, WIDTH, WIDTH, 2 * D_MODEL]
    offs = [0]
    for s in sizes:
        offs.append(offs[-1] + s)
    w_qa, w_ka, w_va, w_fa, w_za, w_qb, w_kb, w_vb, w_zb, w_gate = (
        w[:, offs[n]:offs[n + 1]] for n in range(len(sizes)))
    wq_f = _pad_heads(w_qa, FOX_HEADS).astype(BF16)
    wk_f = _pad_heads(w_ka, FOX_HEADS).astype(BF16)
    wf = jnp.pad(w_fa, ((0, 0), (0, LANES - FOX_HEADS))).astype(BF16)
    bf = jnp.pad(b_forget[0], (0, LANES - FOX_HEADS)).reshape(1, LANES).astype(F32)
    w_nat = jnp.concatenate([w_qb, w_kb, w_zb, w_gate], axis=1).astype(BF16)
    w_tr = jnp.concatenate([w_va, w_za, w_vb], axis=1).T.astype(BF16)
    g_pre2 = g_pre[0].reshape(1, D_MODEL)
    g_post2 = g_post[0].reshape(1, D_MODEL)
    g_sub2 = g_subln[0].reshape(DIFF_V_DIM, 1)
    lamv = jnp.pad(jnp.stack([lambda_q1[0], lambda_k1[0], lambda_q2[0], lambda_k2[0]]),
                   ((0, 0), (0, LANES - DIFF_QK_DIM))).astype(F32)
    wa = w_branch[0, 0].astype(BF16)
    wb = w_branch[0, 1].astype(BF16)
    wo = w_out[0].astype(BF16)
    cos_t, sin_t = _rope_tables(seq)
    x2 = x.reshape(n_rows, D_MODEL)

    x_spec = pl.BlockSpec((TM, D_MODEL), lambda i: (i, 0))
    g_spec = _const_spec((1, D_MODEL))

    qk_shape = jax.ShapeDtypeStruct((batch, FOX_HEADS, seq, LANES), BF16)
    qk_spec = pl.BlockSpec((None, FOX_HEADS, TM, LANES),
                           lambda i: (i // tiles_per_seq, 0, i % tiles_per_seq, 0))
    q_fox, k_fox = pl.pallas_call(
        functools.partial(_fox_qk_kernel, tiles_per_seq),
        out_shape=(qk_shape, qk_shape),
        grid=(n_tiles,),
        in_specs=[x_spec, g_spec,
                  _const_spec(wq_f.shape), _const_spec(wk_f.shape),
                  _const_spec(wf.shape), _const_spec(bf.shape),
                  _const_spec((N_SPLIT * LANES, FOX_HEADS * LANES))],
        out_specs=(qk_spec, qk_spec),
        scratch_shapes=[pltpu.VMEM((1, LANES), F32)],
        compiler_params=pltpu.CompilerParams(
            dimension_semantics=("arbitrary",), vmem_limit_bytes=VMEM_LIMIT),
        name="fox_qk_proj",
    )(x2, g_pre2, wq_f, wk_f, wf, bf, _placement())

    row_spec = lambda n: pl.BlockSpec((TM, n), lambda i: (i, 0))
    rope_spec = pl.BlockSpec((TM, LANES), lambda i: (i % tiles_per_seq, 0))
    q_diff, k_diff, z_diff, gates = pl.pallas_call(
        _nat_kernel,
        out_shape=(jax.ShapeDtypeStruct((n_rows, WIDTH), BF16),
                   jax.ShapeDtypeStruct((n_rows, WIDTH), BF16),
                   jax.ShapeDtypeStruct((n_rows, WIDTH), BF16),
                   jax.ShapeDtypeStruct((n_rows, 2 * D_MODEL), BF16)),
        grid=(n_tiles,),
        in_specs=[x_spec, g_spec, _const_spec(w_nat.shape), rope_spec, rope_spec],
        out_specs=(row_spec(WIDTH), row_spec(WIDTH), row_spec(WIDTH),
                   row_spec(2 * D_MODEL)),
        compiler_params=pltpu.CompilerParams(
            dimension_semantics=("parallel",), vmem_limit_bytes=VMEM_LIMIT),
        name="nat_proj",
    )(x2, g_pre2, w_nat, cos_t, sin_t)

    tr_shape = jax.ShapeDtypeStruct((n_tiles, WIDTH, TM), BF16)
    tr_spec = pl.BlockSpec((None, WIDTH, TM), lambda i: (i, 0, 0))
    v_fox_t, z_fox_t, v_diff_t = pl.pallas_call(
        _tr_kernel,
        out_shape=(tr_shape, tr_shape, tr_shape),
        grid=(n_tiles,),
        in_specs=[x_spec, g_spec, _const_spec(w_tr.shape)],
        out_specs=(tr_spec, tr_spec, tr_spec),
        compiler_params=pltpu.CompilerParams(
            dimension_semantics=("parallel",), vmem_limit_bytes=VMEM_LIMIT),
        name="tr_proj",
    )(x2, g_pre2, w_tr)
    v_fox_t = v_fox_t.reshape(batch, tiles_per_seq, WIDTH, TM)
    z_fox_t = z_fox_t.reshape(batch, tiles_per_seq, WIDTH, TM)
    v_diff_t = v_diff_t.reshape(batch, tiles_per_seq, WIDTH, TM)

    ya_t = pl.pallas_call(
        _fox_attn_kernel,
        out_shape=jax.ShapeDtypeStruct((batch, tiles_per_seq, WIDTH, TM), BF16),
        grid=(batch, FOX_HEADS // 2, seq // TQ),
        in_specs=[
            pl.BlockSpec((None, 2, TQ, LANES), lambda b, h, i: (b, h, i, 0)),
            pl.BlockSpec((None, 2, seq, LANES), lambda b, h, i: (b, h, 0, 0)),
            pl.BlockSpec((None, tiles_per_seq, 2 * FOX_HEAD_DIM, TK),
                         lambda b, h, i: (b, 0, h, 0)),
            pl.BlockSpec((None, TQ // TM, 2 * FOX_HEAD_DIM, TM),
                         lambda b, h, i: (b, i, h, 0)),
        ],
        out_specs=pl.BlockSpec((None, TQ // TM, 2 * FOX_HEAD_DIM, TM),
                               lambda b, h, i: (b, i, h, 0)),
        scratch_shapes=_attn_scratch(FOX_HEAD_DIM),
        compiler_params=_ATTN_PARAMS,
        name="fox_attn",
    )(q_fox, k_fox, v_fox_t, z_fox_t)

    q_diff3 = q_diff.reshape(batch, seq, WIDTH)
    k_diff3 = k_diff.reshape(batch, seq, WIDTH)
    z_diff3 = z_diff.reshape(batch, seq, WIDTH)
    yb = pl.pallas_call(
        _diff_attn_kernel,
        out_shape=jax.ShapeDtypeStruct((batch, seq, WIDTH), BF16),
        grid=(batch, DIFF_HEADS, seq // TQ),
        in_specs=[
            pl.BlockSpec((4, LANES), lambda b, h, i: (0, 0)),
            pl.BlockSpec((None, TQ, LANES), lambda b, h, i: (b, i, h)),
            pl.BlockSpec((None, seq, LANES), lambda b, h, i: (b, 0, h)),
            pl.BlockSpec((None, tiles_per_seq, DIFF_V_DIM, TK),
                         lambda b, h, i: (b, 0, h, 0)),
            pl.BlockSpec((None, TQ, LANES), lambda b, h, i: (b, i, h)),
            pl.BlockSpec((DIFF_V_DIM, 1), lambda b, h, i: (0, 0)),
        ],
        out_specs=pl.BlockSpec((None, TQ, LANES), lambda b, h, i: (b, i, h)),
        scratch_shapes=[pltpu.VMEM((2, TQ, LANES), BF16)] + _attn_scratch(DIFF_V_DIM),
        compiler_params=_ATTN_PARAMS,
        name="diff_attn",
    )(lamv, q_diff3, k_diff3, v_diff_t, z_diff3, g_sub2)

    out = pl.pallas_call(
        _merge_kernel,
        out_shape=jax.ShapeDtypeStruct((n_rows, D_MODEL), F32),
        grid=(n_tiles,),
        in_specs=[
            pl.BlockSpec((None, WIDTH, TM), lambda i: (i, 0, 0)),
            row_spec(WIDTH), row_spec(2 * D_MODEL), x_spec,
            _const_spec(wa.shape), _const_spec(wb.shape), _const_spec(wo.shape),
            g_spec,
        ],
        out_specs=x_spec,
        compiler_params=pltpu.CompilerParams(
            dimension_semantics=("parallel",), vmem_limit_bytes=VMEM_LIMIT),
        name="merge_out",
    )(ya_t.reshape(n_tiles, WIDTH, TM), yb.reshape(n_rows, WIDTH), gates, x2,
      wa, wb, wo, g_post2)
    return out.reshape(batch, seq, D_MODEL)
```

```python
import functools
import math

import jax
import jax.numpy as jnp
from jax import lax
from jax.experimental import pallas as pl
from jax.experimental.pallas import tpu as pltpu

F32 = jnp.float32
BF16 = jnp.bfloat16

D_MODEL = 1024
FOX_HEADS = 8
FOX_HEAD_DIM = 64
DIFF_HEADS = 4
DIFF_QK_DIM = 64
DIFF_V_DIM = 128
WIDTH = 512
ROPE_THETA = 10000.0
NORM_EPS = 1e-6
SUBLN_EPS = 1e-5
LAMBDA_INIT = 0.8 - 0.6 * math.exp(-0.3 * 0)

LANES = 128
TM = 512
TQ = 1024
TK = 512
ONES_ROWS = 16
N_SPLIT = 3
AUG0 = FOX_HEAD_DIM
LOG2E = math.log2(math.e)
QK_SCALE = LOG2E / math.sqrt(FOX_HEAD_DIM)
NEG_BIG = -1e30
VMEM_LIMIT = 48 * 1024 * 1024

_NT = (((1,), (1,)), ((), ()))
_TN = (((0,), (0,)), ((), ()))


def _const_spec(shape):
    return pl.BlockSpec(shape, lambda *_: (0,) * len(shape),
                        pipeline_mode=pl.Buffered(1))


def _rms_rows(x, g, eps):
    ms = jnp.mean(x * x, axis=-1, keepdims=True)
    return x * lax.rsqrt(ms + eps) * g


def _split_bf16(x):
    parts = []
    r = x
    for _ in range(N_SPLIT - 1):
        p = r.astype(BF16)
        parts.append(p)
        r = r - p.astype(F32)
    parts.append(r.astype(BF16))
    return parts


def _log_sigmoid(x):
    return jnp.minimum(x, 0.0) - jnp.log(1.0 + jnp.exp(-jnp.abs(x)))


def _silu(z):
    return z * (1.0 / (1.0 + jnp.exp(-z)))


def _sigmoid(z):
    return 1.0 / (1.0 + jnp.exp(-z))


def _fox_qk_kernel(tiles_per_seq, x_ref, g_ref, wq_ref, wk_ref, wf_ref, bf_ref,
                   place_ref, q_ref, k_ref, carry_ref):
    i = pl.program_id(0)

    @pl.when(i % tiles_per_seq == 0)
    def _():
        carry_ref[...] = jnp.zeros_like(carry_ref)

    h = _rms_rows(x_ref[...], g_ref[...], NORM_EPS).astype(BF16)

    lane = lax.broadcasted_iota(jnp.int32, (1, LANES), 1)
    head_lanes = lane < FOX_HEAD_DIM

    def head_slot(x, hh, extra):
        pair = x[:, (hh // 2) * LANES:(hh // 2 + 1) * LANES]
        if hh % 2:
            pair = pltpu.roll(pair, FOX_HEAD_DIM, axis=1)
        return jnp.where(head_lanes, pair, extra)

    q = jnp.dot(h, wq_ref[...], preferred_element_type=F32) * QK_SCALE
    ones_lanes = jnp.where((lane >= AUG0) & (lane < AUG0 + N_SPLIT), 1.0, 0.0)
    for hh in range(FOX_HEADS):
        q_ref[hh] = head_slot(q, hh, ones_lanes).astype(BF16)

    f = jnp.dot(h, wf_ref[...], preferred_element_type=F32) + bf_ref[...]
    logf = _log_sigmoid(f)
    rows = lax.broadcasted_iota(jnp.int32, (TM, TM), 0)
    cols = lax.broadcasted_iota(jnp.int32, (TM, TM), 1)
    tri = (rows >= cols).astype(BF16)
    c = carry_ref[...]
    for part in _split_bf16(logf):
        c = c + jnp.dot(tri, part, preferred_element_type=F32)
    carry_ref[...] = c[TM - 1:TM, :]

    neg_c = jnp.concatenate(_split_bf16(-LOG2E * c), axis=1)
    k = jnp.dot(h, wk_ref[...], preferred_element_type=F32)
    k_extra = jnp.dot(neg_c, place_ref[...], preferred_element_type=F32)
    for hh in range(FOX_HEADS):
        k_ref[hh] = head_slot(k, hh, k_extra[:, hh * LANES:(hh + 1) * LANES]).astype(BF16)


def _rope(x, cos, sin_signed):
    lane = lax.broadcasted_iota(jnp.int32, (1, LANES), 1) % DIFF_QK_DIM
    first_half = lane < DIFF_QK_DIM // 2
    outs = []
    for g in range(x.shape[1] // LANES):
        xg = x[:, g * LANES:(g + 1) * LANES]
        partner = jnp.where(first_half,
                            pltpu.roll(xg, LANES - DIFF_QK_DIM // 2, axis=1),
                            pltpu.roll(xg, DIFF_QK_DIM // 2, axis=1))
        outs.append(xg * cos + partner * sin_signed)
    return jnp.concatenate(outs, axis=1)


def _nat_kernel(x_ref, g_ref, w_ref, cos_ref, sin_ref, q_ref, k_ref, z_ref, gate_ref):
    h = _rms_rows(x_ref[...], g_ref[...], NORM_EPS).astype(BF16)
    cos = cos_ref[...]
    sin = sin_ref[...]
    q = jnp.dot(h, w_ref[:, 0:WIDTH], preferred_element_type=F32)
    q_ref[...] = (_rope(q, cos, sin) * QK_SCALE).astype(BF16)
    k = jnp.dot(h, w_ref[:, WIDTH:2 * WIDTH], preferred_element_type=F32)
    k_ref[...] = _rope(k, cos, sin).astype(BF16)
    z_ref[...] = jnp.dot(h, w_ref[:, 2 * WIDTH:3 * WIDTH],
                         preferred_element_type=F32).astype(BF16)
    gate_ref[...] = jnp.dot(h, w_ref[:, 3 * WIDTH:],
                            preferred_element_type=F32).astype(BF16)


def _tr_kernel(x_ref, g_ref, wt_ref, vf_ref, zf_ref, vd_ref):
    h = _rms_rows(x_ref[...], g_ref[...], NORM_EPS).astype(BF16)
    for idx, ref in enumerate((vf_ref, zf_ref, vd_ref)):
        w = wt_ref[idx * WIDTH:(idx + 1) * WIDTH, :]
        ref[...] = lax.dot_general(w, h, _NT,
                                   preferred_element_type=F32).astype(BF16)


def _two_chain_attention(i, q_fn, k_fn, vt_fn, rows, acc_ref, m_ref, al_ref, s_ref, bm_ref,
                         p1_ref):
    aug = rows + ONES_ROWS
    ones = jnp.ones((ONES_ROWS, TK), BF16)
    full = slice(0, TQ)
    upper = slice(TQ // 2, TQ)

    def produce(c, j, slot, causal, lanes=full, next_tile=False):
        q = q_fn(c, next_tile)[lanes, :]
        s_t = lax.dot_general(k_fn(c, j), q, _NT, preferred_element_type=F32)
        if causal:
            kpos = lax.broadcasted_iota(jnp.int32, s_t.shape, 0)
            qpos = lax.broadcasted_iota(jnp.int32, s_t.shape, 1)
            s_t = jnp.where(kpos - qpos <= TQ * i + lanes.start - TK * j, s_t, NEG_BIG)
        s_ref[slot, c, :, lanes] = s_t
        bm_ref[slot, c:c + 1, lanes] = jnp.max(s_t, axis=0, keepdims=True)

    def consume(c, slot, lanes):
        m_old = m_ref[c:c + 1, lanes]
        m_new = jnp.maximum(m_old, bm_ref[slot, c:c + 1, lanes])
        m_ref[c:c + 1, lanes] = m_new
        return jnp.exp2(s_ref[slot, c, :, lanes] - m_new).astype(BF16), jnp.exp2(m_old - m_new)

    def accumulate(c, j, lanes):
        vt = jnp.concatenate([vt_fn(c, j), ones], axis=0)
        sl = slice(c * aug, (c + 1) * aug)

        def update(alpha, p):
            acc_ref[sl, lanes] = alpha * acc_ref[sl, lanes] + jnp.dot(
                vt, p, preferred_element_type=F32)
        return update

    def iteration(j, slot, next_block, lanes=full):
        for c in range(2):
            produce(c, *next_block)
        accumulate(1, jnp.maximum(j - 1, 0), full)(al_ref[...], p1_ref[...])
        p0, alpha0 = consume(0, slot, lanes)
        accumulate(0, j, lanes)(alpha0, p0)
        p1, alpha1 = consume(1, slot, lanes)
        p1_ref[:, lanes] = p1
        al_ref[:, lanes] = alpha1

    m_ref[...] = jnp.full_like(m_ref, NEG_BIG)
    acc_ref[...] = jnp.zeros_like(acc_ref)
    p1_ref[...] = jnp.zeros_like(p1_ref)
    al_ref[...] = jnp.ones_like(al_ref)

    @pl.when(i == 0)
    def _():
        for c in range(2):
            produce(c, 0, 0, True)

    last = 2 * i + 1

    def pair(t, carry):
        iteration(2 * t, 0, (2 * t + 1, 1, False))
        iteration(2 * t + 1, 1, (2 * t + 2, 0, False))
        return carry

    lax.fori_loop(0, i - 1, pair, 0)

    @pl.when(i > 0)
    def _():
        iteration(last - 3, 0, (last - 2, 1, False))
        iteration(last - 2, 1, (last - 1, 0, True))

    iteration(last - 1, 0, (last, 1, True, upper))
    iteration(last, 1, (0, 0, False, full, True), upper)
    accumulate(1, last, upper)(al_ref[:, upper], p1_ref[:, upper])


def _kv_rows(j):
    return pl.ds(pl.multiple_of(j * TK, TK), TK)


def _normalized(acc_ref, c, rows):
    base = c * (rows + ONES_ROWS)
    return acc_ref[base:base + rows, :] * (1.0 / acc_ref[base + rows:base + rows + 1, :])


def _fox_attn_kernel(q_ref, qn_ref, k_ref, vt_ref, zt_ref, o_ref, acc_ref, *scratch):
    d = FOX_HEAD_DIM

    def q_fn(c, next_tile):
        return (qn_ref if next_tile else q_ref).at[c]

    def k_fn(c, j):
        return k_ref[c, _kv_rows(j), :]

    def vt_fn(c, j):
        return vt_ref[j, c * d:(c + 1) * d, :]

    _two_chain_attention(pl.program_id(2), q_fn, k_fn, vt_fn, d, acc_ref, *scratch)
    for c in range(2):
        sl = slice(c * d, (c + 1) * d)
        y = _normalized(acc_ref, c, d)
        for t in range(TQ // TM):
            z = zt_ref[t, sl, :].astype(F32)
            o_ref[t, sl, :] = (y[:, t * TM:(t + 1) * TM] * _silu(z)).astype(BF16)


def _diff_attn_kernel(lamv_ref, q_ref, qn_ref, k_ref, vt_ref, z_ref, g_ref, o_ref,
                      qm_ref, acc_ref, *scratch):
    lane = lax.broadcasted_iota(jnp.int32, (1, LANES), 1)
    for t, ref in enumerate((q_ref, qn_ref)):
        q = ref[...]
        zero = jnp.zeros_like(q)
        qm_ref[2 * t] = jnp.where(lane < DIFF_QK_DIM, q, zero)
        qm_ref[2 * t + 1] = jnp.where(lane >= DIFF_QK_DIM, q, zero)

    def q_fn(c, next_tile):
        return qm_ref.at[c + 2 * int(next_tile)]

    def k_fn(c, j):
        return k_ref[_kv_rows(j), :]

    def vt_fn(c, j):
        return vt_ref[j]

    _two_chain_attention(pl.program_id(2), q_fn, k_fn, vt_fn, DIFF_V_DIM, acc_ref, *scratch)

    lamv = lamv_ref[...]
    lam = (jnp.exp(jnp.sum(lamv[0:1] * lamv[1:2], axis=1, keepdims=True))
           - jnp.exp(jnp.sum(lamv[2:3] * lamv[3:4], axis=1, keepdims=True))
           + LAMBDA_INIT)
    o = (_normalized(acc_ref, 0, DIFF_V_DIM)
         - lam * _normalized(acc_ref, 1, DIFF_V_DIM))
    ms = jnp.mean(o * o, axis=0, keepdims=True)
    y = o * lax.rsqrt(ms + SUBLN_EPS) * g_ref[...] * (1.0 - LAMBDA_INIT)
    z = z_ref[...].astype(F32)
    o_ref[...] = (y.T * _silu(z)).astype(BF16)


def _merge_kernel(yat_ref, yb_ref, gate_ref, x_ref, wa_ref, wb_ref, wo_ref, g_ref, o_ref):
    ma = lax.dot_general(yat_ref[...], wa_ref[...], _TN, preferred_element_type=F32)
    mb = jnp.dot(yb_ref[...], wb_ref[...], preferred_element_type=F32)
    ga = _sigmoid(gate_ref[:, 0:D_MODEL].astype(F32))
    gb = _sigmoid(gate_ref[:, D_MODEL:].astype(F32))
    merged = (ga * ma + gb * mb).astype(BF16)
    y = jnp.dot(merged, wo_ref[...], preferred_element_type=F32)
    o_ref[...] = x_ref[...] + _rms_rows(y, g_ref[...], NORM_EPS)


_ATTN_PARAMS = pltpu.CompilerParams(
    dimension_semantics=("arbitrary", "arbitrary", "arbitrary"),
    vmem_limit_bytes=VMEM_LIMIT)


def _attn_scratch(rows):
    return [pltpu.VMEM((2 * (rows + ONES_ROWS), TQ), F32), pltpu.VMEM((2, TQ), F32),
            pltpu.VMEM((1, TQ), F32), pltpu.VMEM((2, 2, TK, TQ), F32),
            pltpu.VMEM((2, 2, TQ), F32), pltpu.VMEM((TK, TQ), BF16)]


def _placement():
    src = jnp.arange(N_SPLIT * LANES)
    term, head = src // LANES, src % LANES
    dst = head * LANES + AUG0 + term
    valid = head < FOX_HEADS
    cols = jnp.arange(FOX_HEADS * LANES)
    return ((cols[None, :] == dst[:, None]) & valid[:, None]).astype(BF16)


def _rope_tables(seq):
    pos = jnp.arange(seq, dtype=F32)
    inv_freq = ROPE_THETA ** (-jnp.arange(0, DIFF_QK_DIM, 2, dtype=F32) / DIFF_QK_DIM)
    ang = pos[:, None] * inv_freq[None, :]
    cos, sin = jnp.cos(ang), jnp.sin(ang)
    reps = LANES // (DIFF_QK_DIM // 2)
    cos_t = jnp.tile(cos, (1, reps))
    sin_t = jnp.tile(jnp.concatenate([-sin, sin], axis=1), (1, reps // 2))
    return cos_t, sin_t


def kernel(x, g_pre, w_in, b_forget, lambda_q1, lambda_k1, lambda_q2, lambda_k2,
           g_subln, w_branch, w_out, g_post):
    batch, seq, d = x.shape
    assert d == D_MODEL and seq % TQ == 0 and TM == TK and TQ == 2 * TK
    n_rows = batch * seq
    n_tiles = n_rows // TM
    tiles_per_seq = seq // TM

    w = w_in[0]
    sizes = [WIDTH, WIDTH, WIDTH, FOX_HEADS, WIDTH, WIDTH, WIDTH, WIDTH, WIDTH, 2 * D_MODEL]
    offs = [0]
    for s in sizes:
        offs.append(offs[-1] + s)
    w_qa, w_ka, w_va, w_fa, w_za, w_qb, w_kb, w_vb, w_zb, w_gate = (
        w[:, offs[n]:offs[n + 1]] for n in range(len(sizes)))
    wq_f = w_qa.astype(BF16)
    wk_f = w_ka.astype(BF16)
    wf = jnp.pad(w_fa, ((0, 0), (0, LANES - FOX_HEADS))).astype(BF16)
    bf = jnp.pad(b_forget[0], (0, LANES - FOX_HEADS)).reshape(1, LANES).astype(F32)
    w_nat = jnp.concatenate([w_qb, w_kb, w_zb, w_gate], axis=1).astype(BF16)
    w_tr = jnp.concatenate([w_va, w_za, w_vb], axis=1).T.astype(BF16)
    g_pre2 = g_pre[0].reshape(1, D_MODEL)
    g_post2 = g_post[0].reshape(1, D_MODEL)
    g_sub2 = g_subln[0].reshape(DIFF_V_DIM, 1)
    lamv = jnp.pad(jnp.stack([lambda_q1[0], lambda_k1[0], lambda_q2[0], lambda_k2[0]]),
                   ((0, 0), (0, LANES - DIFF_QK_DIM))).astype(F32)
    wa = w_branch[0, 0].astype(BF16)
    wb = w_branch[0, 1].astype(BF16)
    wo = w_out[0].astype(BF16)
    cos_t, sin_t = _rope_tables(seq)
    x2 = x.reshape(n_rows, D_MODEL)

    x_spec = pl.BlockSpec((TM, D_MODEL), lambda i: (i, 0))
    g_spec = _const_spec((1, D_MODEL))

    qk_shape = jax.ShapeDtypeStruct((batch, FOX_HEADS, seq, LANES), BF16)
    qk_spec = pl.BlockSpec((None, FOX_HEADS, TM, LANES),
                           lambda i: (i // tiles_per_seq, 0, i % tiles_per_seq, 0))
    q_fox, k_fox = pl.pallas_call(
        functools.partial(_fox_qk_kernel, tiles_per_seq),
        out_shape=(qk_shape, qk_shape),
        grid=(n_tiles,),
        in_specs=[x_spec, g_spec,
                  _const_spec(wq_f.shape), _const_spec(wk_f.shape),
                  _const_spec(wf.shape), _const_spec(bf.shape),
                  _const_spec((N_SPLIT * LANES, FOX_HEADS * LANES))],
        out_specs=(qk_spec, qk_spec),
        scratch_shapes=[pltpu.VMEM((1, LANES), F32)],
        compiler_params=pltpu.CompilerParams(
            dimension_semantics=("arbitrary",), vmem_limit_bytes=VMEM_LIMIT),
        name="fox_qk_proj",
    )(x2, g_pre2, wq_f, wk_f, wf, bf, _placement())

    row_spec = lambda n: pl.BlockSpec((TM, n), lambda i: (i, 0))
    rope_spec = pl.BlockSpec((TM, LANES), lambda i: (i % tiles_per_seq, 0))
    q_diff, k_diff, z_diff, gates = pl.pallas_call(
        _nat_kernel,
        out_shape=(jax.ShapeDtypeStruct((n_rows, WIDTH), BF16),
                   jax.ShapeDtypeStruct((n_rows, WIDTH), BF16),
                   jax.ShapeDtypeStruct((n_rows, WIDTH), BF16),
                   jax.ShapeDtypeStruct((n_rows, 2 * D_MODEL), BF16)),
        grid=(n_tiles,),
        in_specs=[x_spec, g_spec, _const_spec(w_nat.shape), rope_spec, rope_spec],
        out_specs=(row_spec(WIDTH), row_spec(WIDTH), row_spec(WIDTH),
                   row_spec(2 * D_MODEL)),
        compiler_params=pltpu.CompilerParams(
            dimension_semantics=("parallel",), vmem_limit_bytes=VMEM_LIMIT),
        name="nat_proj",
    )(x2, g_pre2, w_nat, cos_t, sin_t)

    tr_shape = jax.ShapeDtypeStruct((n_tiles, WIDTH, TM), BF16)
    tr_spec = pl.BlockSpec((None, WIDTH, TM), lambda i: (i, 0, 0))
    v_fox_t, z_fox_t, v_diff_t = pl.pallas_call(
        _tr_kernel,
        out_shape=(tr_shape, tr_shape, tr_shape),
        grid=(n_tiles,),
        in_specs=[x_spec, g_spec, _const_spec(w_tr.shape)],
        out_specs=(tr_spec, tr_spec, tr_spec),
        compiler_params=pltpu.CompilerParams(
            dimension_semantics=("parallel",), vmem_limit_bytes=VMEM_LIMIT),
        name="tr_proj",
    )(x2, g_pre2, w_tr)
    v_fox_t = v_fox_t.reshape(batch, tiles_per_seq, WIDTH, TM)
    z_fox_t = z_fox_t.reshape(batch, tiles_per_seq, WIDTH, TM)
    v_diff_t = v_diff_t.reshape(batch, tiles_per_seq, WIDTH, TM)

    def next_tile(i):
        return jnp.minimum(i + 1, seq // TQ - 1)

    ya_t = pl.pallas_call(
        _fox_attn_kernel,
        out_shape=jax.ShapeDtypeStruct((batch, tiles_per_seq, WIDTH, TM), BF16),
        grid=(batch, FOX_HEADS // 2, seq // TQ),
        in_specs=[
            pl.BlockSpec((None, 2, TQ, LANES), lambda b, h, i: (b, h, i, 0)),
            pl.BlockSpec((None, 2, TQ, LANES), lambda b, h, i: (b, h, next_tile(i), 0)),
            pl.BlockSpec((None, 2, seq, LANES), lambda b, h, i: (b, h, 0, 0)),
            pl.BlockSpec((None, tiles_per_seq, 2 * FOX_HEAD_DIM, TK),
                         lambda b, h, i: (b, 0, h, 0)),
            pl.BlockSpec((None, TQ // TM, 2 * FOX_HEAD_DIM, TM),
                         lambda b, h, i: (b, i, h, 0)),
        ],
        out_specs=pl.BlockSpec((None, TQ // TM, 2 * FOX_HEAD_DIM, TM),
                               lambda b, h, i: (b, i, h, 0)),
        scratch_shapes=_attn_scratch(FOX_HEAD_DIM),
        compiler_params=_ATTN_PARAMS,
        name="fox_attn",
    )(q_fox, q_fox, k_fox, v_fox_t, z_fox_t)

    q_diff3 = q_diff.reshape(batch, seq, WIDTH)
    k_diff3 = k_diff.reshape(batch, seq, WIDTH)
    z_diff3 = z_diff.reshape(batch, seq, WIDTH)
    yb = pl.pallas_call(
        _diff_attn_kernel,
        out_shape=jax.ShapeDtypeStruct((batch, seq, WIDTH), BF16),
        grid=(batch, DIFF_HEADS, seq // TQ),
        in_specs=[
            pl.BlockSpec((4, LANES), lambda b, h, i: (0, 0)),
            pl.BlockSpec((None, TQ, LANES), lambda b, h, i: (b, i, h)),
            pl.BlockSpec((None, TQ, LANES), lambda b, h, i: (b, next_tile(i), h)),
            pl.BlockSpec((None, seq, LANES), lambda b, h, i: (b, 0, h)),
            pl.BlockSpec((None, tiles_per_seq, DIFF_V_DIM, TK),
                         lambda b, h, i: (b, 0, h, 0)),
            pl.BlockSpec((None, TQ, LANES), lambda b, h, i: (b, i, h)),
            pl.BlockSpec((DIFF_V_DIM, 1), lambda b, h, i: (0, 0)),
        ],
        out_specs=pl.BlockSpec((None, TQ, LANES), lambda b, h, i: (b, i, h)),
        scratch_shapes=[pltpu.VMEM((4, TQ, LANES), BF16)] + _attn_scratch(DIFF_V_DIM),
        compiler_params=_ATTN_PARAMS,
        name="diff_attn",
    )(lamv, q_diff3, q_diff3, k_diff3, v_diff_t, z_diff3, g_sub2)

    out = pl.pallas_call(
        _merge_kernel,
        out_shape=jax.ShapeDtypeStruct((n_rows, D_MODEL), F32),
        grid=(n_tiles,),
        in_specs=[
            pl.BlockSpec((None, WIDTH, TM), lambda i: (i, 0, 0)),
            row_spec(WIDTH), row_spec(2 * D_MODEL), x_spec,
            _const_spec(wa.shape), _const_spec(wb.shape), _const_spec(wo.shape),
            g_spec,
        ],
        out_specs=x_spec,
        compiler_params=pltpu.CompilerParams(
            dimension_semantics=("parallel",), vmem_limit_bytes=VMEM_LIMIT),
        name="merge_out",
    )(ya_t.reshape(n_tiles, WIDTH, TM), yb.reshape(n_rows, WIDTH), gates, x2,
      wa, wb, wo, g_post2)
    return out.reshape(batch, seq, D_MODEL)
```

```python
import functools
import math

import jax
import jax.numpy as jnp
from jax import lax
from jax.experimental import pallas as pl
from jax.experimental.pallas import tpu as pltpu

F32 = jnp.float32
BF16 = jnp.bfloat16

D_MODEL = 1024
FOX_HEADS = 8
FOX_HEAD_DIM = 64
DIFF_HEADS = 4
DIFF_QK_DIM = 64
DIFF_V_DIM = 128
WIDTH = 512
ROPE_THETA = 10000.0
NORM_EPS = 1e-6
SUBLN_EPS = 1e-5
LAMBDA_INIT = 0.8 - 0.6 * math.exp(-0.3 * 0)

LANES = 128
TM = 512
TQ = 1024
TK = 512
ONES_ROWS = 16
N_SPLIT = 3
AUG0 = FOX_HEAD_DIM
LOG2E = math.log2(math.e)
QK_SCALE = LOG2E / math.sqrt(FOX_HEAD_DIM)
NEG_BIG = -1e30
VMEM_LIMIT = 48 * 1024 * 1024

_NT = (((1,), (1,)), ((), ()))
_TN = (((0,), (0,)), ((), ()))


def _const_spec(shape):
    return pl.BlockSpec(shape, lambda *_: (0,) * len(shape),
                        pipeline_mode=pl.Buffered(1))


def _rms_rows(x, g, eps):
    ms = jnp.mean(x * x, axis=-1, keepdims=True)
    return x * lax.rsqrt(ms + eps) * g


def _split_bf16(x):
    parts = []
    r = x
    for _ in range(N_SPLIT - 1):
        p = r.astype(BF16)
        parts.append(p)
        r = r - p.astype(F32)
    parts.append(r.astype(BF16))
    return parts


def _log_sigmoid(x):
    return jnp.minimum(x, 0.0) - jnp.log(1.0 + jnp.exp(-jnp.abs(x)))


def _silu(z):
    return z * (1.0 / (1.0 + jnp.exp(-z)))


def _sigmoid(z):
    return 1.0 / (1.0 + jnp.exp(-z))


def _fox_qk_kernel(tiles_per_seq, x_ref, g_ref, wq_ref, wk_ref, wf_ref, bf_ref,
                   place_ref, q_ref, k_ref, carry_ref):
    i = pl.program_id(0)

    @pl.when(i % tiles_per_seq == 0)
    def _():
        carry_ref[...] = jnp.zeros_like(carry_ref)

    h = _rms_rows(x_ref[...], g_ref[...], NORM_EPS).astype(BF16)

    lane = lax.broadcasted_iota(jnp.int32, (1, LANES), 1)
    head_lanes = lane < FOX_HEAD_DIM

    def head_slot(x, hh, extra):
        pair = x[:, (hh // 2) * LANES:(hh // 2 + 1) * LANES]
        if hh % 2:
            pair = pltpu.roll(pair, FOX_HEAD_DIM, axis=1)
        return jnp.where(head_lanes, pair, extra)

    q = jnp.dot(h, wq_ref[...], preferred_element_type=F32) * QK_SCALE
    ones_lanes = jnp.where((lane >= AUG0) & (lane < AUG0 + N_SPLIT), 1.0, 0.0)
    for hh in range(FOX_HEADS):
        q_ref[hh] = head_slot(q, hh, ones_lanes).astype(BF16)

    f = jnp.dot(h, wf_ref[...], preferred_element_type=F32) + bf_ref[...]
    logf = _log_sigmoid(f)
    rows = lax.broadcasted_iota(jnp.int32, (TM, TM), 0)
    cols = lax.broadcasted_iota(jnp.int32, (TM, TM), 1)
    tri = (rows >= cols).astype(BF16)
    c = carry_ref[...]
    for part in _split_bf16(logf):
        c = c + jnp.dot(tri, part, preferred_element_type=F32)
    carry_ref[...] = c[TM - 1:TM, :]

    neg_c = jnp.concatenate(_split_bf16(-LOG2E * c), axis=1)
    k = jnp.dot(h, wk_ref[...], preferred_element_type=F32)
    k_extra = jnp.dot(neg_c, place_ref[...], preferred_element_type=F32)
    for hh in range(FOX_HEADS):
        k_ref[hh] = head_slot(k, hh, k_extra[:, hh * LANES:(hh + 1) * LANES]).astype(BF16)


def _rope(x, cos, sin_signed):
    lane = lax.broadcasted_iota(jnp.int32, (1, LANES), 1) % DIFF_QK_DIM
    first_half = lane < DIFF_QK_DIM // 2
    outs = []
    for g in range(x.shape[1] // LANES):
        xg = x[:, g * LANES:(g + 1) * LANES]
        partner = jnp.where(first_half,
                            pltpu.roll(xg, LANES - DIFF_QK_DIM // 2, axis=1),
                            pltpu.roll(xg, DIFF_QK_DIM // 2, axis=1))
        outs.append(xg * cos + partner * sin_signed)
    return jnp.concatenate(outs, axis=1)


def _nat_kernel(x_ref, g_ref, w_ref, cos_ref, sin_ref, q_ref, k_ref, z_ref, gate_ref):
    h = _rms_rows(x_ref[...], g_ref[...], NORM_EPS).astype(BF16)
    cos = cos_ref[...]
    sin = sin_ref[...]
    q = jnp.dot(h, w_ref[:, 0:WIDTH], preferred_element_type=F32)
    q_ref[...] = (_rope(q, cos, sin) * QK_SCALE).astype(BF16)
    k = jnp.dot(h, w_ref[:, WIDTH:2 * WIDTH], preferred_element_type=F32)
    k_ref[...] = _rope(k, cos, sin).astype(BF16)
    z_ref[...] = jnp.dot(h, w_ref[:, 2 * WIDTH:3 * WIDTH],
                         preferred_element_type=F32).astype(BF16)
    gate_ref[...] = jnp.dot(h, w_ref[:, 3 * WIDTH:],
                            preferred_element_type=F32).astype(BF16)


def _tr_kernel(x_ref, g_ref, wt_ref, vf_ref, zf_ref, vd_ref):
    h = _rms_rows(x_ref[...], g_ref[...], NORM_EPS).astype(BF16)
    for idx, ref in enumerate((vf_ref, zf_ref, vd_ref)):
        w = wt_ref[idx * WIDTH:(idx + 1) * WIDTH, :]
        ref[...] = lax.dot_general(w, h, _NT,
                                   preferred_element_type=F32).astype(BF16)


def _two_chain_attention(i, q_fn, k_fn, vt_fn, rows, acc_ref, m_ref, al_ref, s_ref, bm_ref,
                         p1_ref):
    aug = rows + ONES_ROWS
    ones = jnp.ones((ONES_ROWS, TK), BF16)
    full = slice(0, TQ)
    upper = slice(TQ // 2, TQ)

    def produce(c, j, slot, causal, lanes=full, next_tile=False):
        q = q_fn(c, next_tile)[lanes, :]
        s_t = lax.dot_general(k_fn(c, j), q, _NT, preferred_element_type=F32)
        if causal:
            kpos = lax.broadcasted_iota(jnp.int32, s_t.shape, 0)
            qpos = lax.broadcasted_iota(jnp.int32, s_t.shape, 1)
            s_t = jnp.where(kpos - qpos <= TQ * i + lanes.start - TK * j, s_t, NEG_BIG)
        s_ref[slot, c, :, lanes] = s_t
        bm_ref[slot, c:c + 1, lanes] = jnp.max(s_t, axis=0, keepdims=True)

    def consume(c, slot, lanes):
        m_old = m_ref[c:c + 1, lanes]
        m_new = jnp.maximum(m_old, bm_ref[slot, c:c + 1, lanes])
        m_ref[c:c + 1, lanes] = m_new
        return jnp.exp2(s_ref[slot, c, :, lanes] - m_new).astype(BF16), jnp.exp2(m_old - m_new)

    def accumulate(c, j, lanes):
        vt = jnp.concatenate([vt_fn(c, j), ones], axis=0)
        sl = slice(c * aug, (c + 1) * aug)

        def update(alpha, p):
            acc_ref[sl, lanes] = alpha * acc_ref[sl, lanes] + jnp.dot(
                vt, p, preferred_element_type=F32)
        return update

    def iteration(j, slot, next_block, lanes=full):
        accumulate(1, jnp.maximum(j - 1, 0), full)(al_ref[...], p1_ref[...])
        produce(0, *next_block)
        p0, alpha0 = consume(0, slot, lanes)
        produce(1, *next_block)
        accumulate(0, j, lanes)(alpha0, p0)
        p1, alpha1 = consume(1, slot, lanes)
        p1_ref[:, lanes] = p1
        al_ref[:, lanes] = alpha1

    m_ref[...] = jnp.full_like(m_ref, NEG_BIG)
    acc_ref[...] = jnp.zeros_like(acc_ref)
    p1_ref[...] = jnp.zeros_like(p1_ref)
    al_ref[...] = jnp.ones_like(al_ref)

    @pl.when(i == 0)
    def _():
        for c in range(2):
            produce(c, 0, 0, True)

    last = 2 * i + 1

    def pair(t, carry):
        iteration(2 * t, 0, (2 * t + 1, 1, False))
        iteration(2 * t + 1, 1, (2 * t + 2, 0, False))
        return carry

    lax.fori_loop(0, i - 1, pair, 0)

    @pl.when(i > 0)
    def _():
        iteration(last - 3, 0, (last - 2, 1, False))
        iteration(last - 2, 1, (last - 1, 0, True))

    iteration(last - 1, 0, (last, 1, True, upper))
    iteration(last, 1, (0, 0, False, full, True), upper)
    accumulate(1, last, upper)(al_ref[:, upper], p1_ref[:, upper])


def _kv_rows(j):
    return pl.ds(pl.multiple_of(j * TK, TK), TK)


def _normalized(acc_ref, c, rows):
    base = c * (rows + ONES_ROWS)
    return acc_ref[base:base + rows, :] * (1.0 / acc_ref[base + rows:base + rows + 1, :])


def _fox_attn_kernel(q_ref, qn_ref, k_ref, vt_ref, zt_ref, o_ref, acc_ref, *scratch):
    d = FOX_HEAD_DIM

    def q_fn(c, next_tile):
        return (qn_ref if next_tile else q_ref).at[c]

    def k_fn(c, j):
        return k_ref[c, _kv_rows(j), :]

    def vt_fn(c, j):
        return vt_ref[j, c * d:(c + 1) * d, :]

    _two_chain_attention(pl.program_id(2), q_fn, k_fn, vt_fn, d, acc_ref, *scratch)
    for c in range(2):
        sl = slice(c * d, (c + 1) * d)
        y = _normalized(acc_ref, c, d)
        for t in range(TQ // TM):
            z = zt_ref[t, sl, :].astype(F32)
            o_ref[t, sl, :] = (y[:, t * TM:(t + 1) * TM] * _silu(z)).astype(BF16)


def _diff_attn_kernel(lamv_ref, q_ref, qn_ref, k_ref, vt_ref, z_ref, g_ref, o_ref,
                      qm_ref, acc_ref, *scratch):
    lane = lax.broadcasted_iota(jnp.int32, (1, LANES), 1)
    for t, ref in enumerate((q_ref, qn_ref)):
        q = ref[...]
        zero = jnp.zeros_like(q)
        qm_ref[2 * t] = jnp.where(lane < DIFF_QK_DIM, q, zero)
        qm_ref[2 * t + 1] = jnp.where(lane >= DIFF_QK_DIM, q, zero)

    def q_fn(c, next_tile):
        return qm_ref.at[c + 2 * int(next_tile)]

    def k_fn(c, j):
        return k_ref[_kv_rows(j), :]

    def vt_fn(c, j):
        return vt_ref[j]

    _two_chain_attention(pl.program_id(2), q_fn, k_fn, vt_fn, DIFF_V_DIM, acc_ref, *scratch)

    lamv = lamv_ref[...]
    lam = (jnp.exp(jnp.sum(lamv[0:1] * lamv[1:2], axis=1, keepdims=True))
           - jnp.exp(jnp.sum(lamv[2:3] * lamv[3:4], axis=1, keepdims=True))
           + LAMBDA_INIT)
    o = (_normalized(acc_ref, 0, DIFF_V_DIM)
         - lam * _normalized(acc_ref, 1, DIFF_V_DIM))
    ms = jnp.mean(o * o, axis=0, keepdims=True)
    y = o * lax.rsqrt(ms + SUBLN_EPS) * g_ref[...] * (1.0 - LAMBDA_INIT)
    z = z_ref[...].astype(F32)
    o_ref[...] = (y.T * _silu(z)).astype(BF16)


def _merge_kernel(yat_ref, yb_ref, gate_ref, x_ref, wa_ref, wb_ref, wo_ref, g_ref, o_ref):
    ma = lax.dot_general(yat_ref[...], wa_ref[...], _TN, preferred_element_type=F32)
    mb = jnp.dot(yb_ref[...], wb_ref[...], preferred_element_type=F32)
    ga = _sigmoid(gate_ref[:, 0:D_MODEL].astype(F32))
    gb = _sigmoid(gate_ref[:, D_MODEL:].astype(F32))
    merged = (ga * ma + gb * mb).astype(BF16)
    y = jnp.dot(merged, wo_ref[...], preferred_element_type=F32)
    o_ref[...] = x_ref[...] + _rms_rows(y, g_ref[...], NORM_EPS)


_ATTN_PARAMS = pltpu.CompilerParams(
    dimension_semantics=("arbitrary", "arbitrary", "arbitrary"),
    vmem_limit_bytes=VMEM_LIMIT)


def _attn_scratch(rows):
    return [pltpu.VMEM((2 * (rows + ONES_ROWS), TQ), F32), pltpu.VMEM((2, TQ), F32),
            pltpu.VMEM((1, TQ), F32), pltpu.VMEM((2, 2, TK, TQ), F32),
            pltpu.VMEM((2, 2, TQ), F32), pltpu.VMEM((TK, TQ), BF16)]


def _placement():
    src = jnp.arange(N_SPLIT * LANES)
    term, head = src // LANES, src % LANES
    dst = head * LANES + AUG0 + term
    valid = head < FOX_HEADS
    cols = jnp.arange(FOX_HEADS * LANES)
    return ((cols[None, :] == dst[:, None]) & valid[:, None]).astype(BF16)


def _rope_tables(seq):
    pos = jnp.arange(seq, dtype=F32)
    inv_freq = ROPE_THETA ** (-jnp.arange(0, DIFF_QK_DIM, 2, dtype=F32) / DIFF_QK_DIM)
    ang = pos[:, None] * inv_freq[None, :]
    cos, sin = jnp.cos(ang), jnp.sin(ang)
    reps = LANES // (DIFF_QK_DIM // 2)
    cos_t = jnp.tile(cos, (1, reps))
    sin_t = jnp.tile(jnp.concatenate([-sin, sin], axis=1), (1, reps // 2))
    return cos_t, sin_t


def kernel(x, g_pre, w_in, b_forget, lambda_q1, lambda_k1, lambda_q2, lambda_k2,
           g_subln, w_branch, w_out, g_post):
    batch, seq, d = x.shape
    assert d == D_MODEL and seq % TQ == 0 and TM == TK and TQ == 2 * TK
    n_rows = batch * seq
    n_tiles = n_rows // TM
    tiles_per_seq = seq // TM

    w = w_in[0]
    sizes = [WIDTH, WIDTH, WIDTH, FOX_HEADS, WIDTH, WIDTH, WIDTH, WIDTH, WIDTH, 2 * D_MODEL]
    offs = [0]
    for s in sizes:
        offs.append(offs[-1] + s)
    w_qa, w_ka, w_va, w_fa, w_za, w_qb, w_kb, w_vb, w_zb, w_gate = (
        w[:, offs[n]:offs[n + 1]] for n in range(len(sizes)))
    wq_f = w_qa.astype(BF16)
    wk_f = w_ka.astype(BF16)
    wf = jnp.pad(w_fa, ((0, 0), (0, LANES - FOX_HEADS))).astype(BF16)
    bf = jnp.pad(b_forget[0], (0, LANES - FOX_HEADS)).reshape(1, LANES).astype(F32)
    w_nat = jnp.concatenate([w_qb, w_kb, w_zb, w_gate], axis=1).astype(BF16)
    w_tr = jnp.concatenate([w_va, w_za, w_vb], axis=1).T.astype(BF16)
    g_pre2 = g_pre[0].reshape(1, D_MODEL)
    g_post2 = g_post[0].reshape(1, D_MODEL)
    g_sub2 = g_subln[0].reshape(DIFF_V_DIM, 1)
    lamv = jnp.pad(jnp.stack([lambda_q1[0], lambda_k1[0], lambda_q2[0], lambda_k2[0]]),
                   ((0, 0), (0, LANES - DIFF_QK_DIM))).astype(F32)
    wa = w_branch[0, 0].astype(BF16)
    wb = w_branch[0, 1].astype(BF16)
    wo = w_out[0].astype(BF16)
    cos_t, sin_t = _rope_tables(seq)
    x2 = x.reshape(n_rows, D_MODEL)

    x_spec = pl.BlockSpec((TM, D_MODEL), lambda i: (i, 0))
    g_spec = _const_spec((1, D_MODEL))

    qk_shape = jax.ShapeDtypeStruct((batch, FOX_HEADS, seq, LANES), BF16)
    qk_spec = pl.BlockSpec((None, FOX_HEADS, TM, LANES),
                           lambda i: (i // tiles_per_seq, 0, i % tiles_per_seq, 0))
    q_fox, k_fox = pl.pallas_call(
        functools.partial(_fox_qk_kernel, tiles_per_seq),
        out_shape=(qk_shape, qk_shape),
        grid=(n_tiles,),
        in_specs=[x_spec, g_spec,
                  _const_spec(wq_f.shape), _const_spec(wk_f.shape),
                  _const_spec(wf.shape), _const_spec(bf.shape),
                  _const_spec((N_SPLIT * LANES, FOX_HEADS * LANES))],
        out_specs=(qk_spec, qk_spec),
        scratch_shapes=[pltpu.VMEM((1, LANES), F32)],
        compiler_params=pltpu.CompilerParams(
            dimension_semantics=("arbitrary",), vmem_limit_bytes=VMEM_LIMIT),
        name="fox_qk_proj",
    )(x2, g_pre2, wq_f, wk_f, wf, bf, _placement())

    row_spec = lambda n: pl.BlockSpec((TM, n), lambda i: (i, 0))
    rope_spec = pl.BlockSpec((TM, LANES), lambda i: (i % tiles_per_seq, 0))
    q_diff, k_diff, z_diff, gates = pl.pallas_call(
        _nat_kernel,
        out_shape=(jax.ShapeDtypeStruct((n_rows, WIDTH), BF16),
                   jax.ShapeDtypeStruct((n_rows, WIDTH), BF16),
                   jax.ShapeDtypeStruct((n_rows, WIDTH), BF16),
                   jax.ShapeDtypeStruct((n_rows, 2 * D_MODEL), BF16)),
        grid=(n_tiles,),
        in_specs=[x_spec, g_spec, _const_spec(w_nat.shape), rope_spec, rope_spec],
        out_specs=(row_spec(WIDTH), row_spec(WIDTH), row_spec(WIDTH),
                   row_spec(2 * D_MODEL)),
        compiler_params=pltpu.CompilerParams(
            dimension_semantics=("parallel",), vmem_limit_bytes=VMEM_LIMIT),
        name="nat_proj",
    )(x2, g_pre2, w_nat, cos_t, sin_t)

    tr_shape = jax.ShapeDtypeStruct((n_tiles, WIDTH, TM), BF16)
    tr_spec = pl.BlockSpec((None, WIDTH, TM), lambda i: (i, 0, 0))
    v_fox_t, z_fox_t, v_diff_t = pl.pallas_call(
        _tr_kernel,
        out_shape=(tr_shape, tr_shape, tr_shape),
        grid=(n_tiles,),
        in_specs=[x_spec, g_spec, _const_spec(w_tr.shape)],
        out_specs=(tr_spec, tr_spec, tr_spec),
        compiler_params=pltpu.CompilerParams(
            dimension_semantics=("parallel",), vmem_limit_bytes=VMEM_LIMIT),
        name="tr_proj",
    )(x2, g_pre2, w_tr)
    v_fox_t = v_fox_t.reshape(batch, tiles_per_seq, WIDTH, TM)
    z_fox_t = z_fox_t.reshape(batch, tiles_per_seq, WIDTH, TM)
    v_diff_t = v_diff_t.reshape(batch, tiles_per_seq, WIDTH, TM)

    def next_tile(i):
        return jnp.minimum(i + 1, seq // TQ - 1)

    ya_t = pl.pallas_call(
        _fox_attn_kernel,
        out_shape=jax.ShapeDtypeStruct((batch, tiles_per_seq, WIDTH, TM), BF16),
        grid=(batch, FOX_HEADS // 2, seq // TQ),
        in_specs=[
            pl.BlockSpec((None, 2, TQ, LANES), lambda b, h, i: (b, h, i, 0)),
            pl.BlockSpec((None, 2, TQ, LANES), lambda b, h, i: (b, h, next_tile(i), 0)),
            pl.BlockSpec((None, 2, seq, LANES), lambda b, h, i: (b, h, 0, 0)),
            pl.BlockSpec((None, tiles_per_seq, 2 * FOX_HEAD_DIM, TK),
                         lambda b, h, i: (b, 0, h, 0)),
            pl.BlockSpec((None, TQ // TM, 2 * FOX_HEAD_DIM, TM),
                         lambda b, h, i: (b, i, h, 0)),
        ],
        out_specs=pl.BlockSpec((None, TQ // TM, 2 * FOX_HEAD_DIM, TM),
                               lambda b, h, i: (b, i, h, 0)),
        scratch_shapes=_attn_scratch(FOX_HEAD_DIM),
        compiler_params=_ATTN_PARAMS,
        name="fox_attn",
    )(q_fox, q_fox, k_fox, v_fox_t, z_fox_t)

    q_diff3 = q_diff.reshape(batch, seq, WIDTH)
    k_diff3 = k_diff.reshape(batch, seq, WIDTH)
    z_diff3 = z_diff.reshape(batch, seq, WIDTH)
    yb = pl.pallas_call(
        _diff_attn_kernel,
        out_shape=jax.ShapeDtypeStruct((batch, seq, WIDTH), BF16),
        grid=(batch, DIFF_HEADS, seq // TQ),
        in_specs=[
            pl.BlockSpec((4, LANES), lambda b, h, i: (0, 0)),
            pl.BlockSpec((None, TQ, LANES), lambda b, h, i: (b, i, h)),
            pl.BlockSpec((None, TQ, LANES), lambda b, h, i: (b, next_tile(i), h)),
            pl.BlockSpec((None, seq, LANES), lambda b, h, i: (b, 0, h)),
            pl.BlockSpec((None, tiles_per_seq, DIFF_V_DIM, TK),
                         lambda b, h, i: (b, 0, h, 0)),
            pl.BlockSpec((None, TQ, LANES), lambda b, h, i: (b, i, h)),
            pl.BlockSpec((DIFF_V_DIM, 1), lambda b, h, i: (0, 0)),
        ],
        out_specs=pl.BlockSpec((None, TQ, LANES), lambda b, h, i: (b, i, h)),
        scratch_shapes=[pltpu.VMEM((4, TQ, LANES), BF16)] + _attn_scratch(DIFF_V_DIM),
        compiler_params=_ATTN_PARAMS,
        name="diff_attn",
    )(lamv, q_diff3, q_diff3, k_diff3, v_diff_t, z_diff3, g_sub2)

    out = pl.pallas_call(
        _merge_kernel,
        out_shape=jax.ShapeDtypeStruct((n_rows, D_MODEL), F32),
        grid=(n_tiles,),
        in_specs=[
            pl.BlockSpec((None, WIDTH, TM), lambda i: (i, 0, 0)),
            row_spec(WIDTH), row_spec(2 * D_MODEL), x_spec,
            _const_spec(wa.shape), _const_spec(wb.shape), _const_spec(wo.shape),
            g_spec,
        ],
        out_specs=x_spec,
        compiler_params=pltpu.CompilerParams(
            dimension_semantics=("parallel",), vmem_limit_bytes=VMEM_LIMIT),
        name="merge_out",
    )(ya_t.reshape(n_tiles, WIDTH, TM), yb.reshape(n_rows, WIDTH), gates, x2,
      wa, wb, wo, g_post2)
    return out.reshape(batch, seq, D_MODEL)
```

```python
import functools
import math

import jax
import jax.numpy as jnp
from jax import lax
from jax.experimental import pallas as pl
from jax.experimental.pallas import tpu as pltpu

F32 = jnp.float32
BF16 = jnp.bfloat16

D_MODEL = 1024
FOX_HEADS = 8
FOX_HEAD_DIM = 64
DIFF_HEADS = 4
DIFF_QK_DIM = 64
DIFF_V_DIM = 128
WIDTH = 512
ROPE_THETA = 10000.0
NORM_EPS = 1e-6
SUBLN_EPS = 1e-5
LAMBDA_INIT = 0.8 - 0.6 * math.exp(-0.3 * 0)

LANES = 128
TM = 512
TQ = 1024
TK = 512
ONES_ROWS = 16
N_SPLIT = 3
AUG0 = FOX_HEAD_DIM
LOG2E = math.log2(math.e)
QK_SCALE = LOG2E / math.sqrt(FOX_HEAD_DIM)
NEG_BIG = -1e30
VMEM_LIMIT = 48 * 1024 * 1024

_NT = (((1,), (1,)), ((), ()))
_TN = (((0,), (0,)), ((), ()))


def _const_spec(shape):
    return pl.BlockSpec(shape, lambda *_: (0,) * len(shape),
                        pipeline_mode=pl.Buffered(1))


def _rms_rows(x, g, eps):
    ms = jnp.mean(x * x, axis=-1, keepdims=True)
    return x * lax.rsqrt(ms + eps) * g


def _split_bf16(x):
    parts = []
    r = x
    for _ in range(N_SPLIT - 1):
        p = r.astype(BF16)
        parts.append(p)
        r = r - p.astype(F32)
    parts.append(r.astype(BF16))
    return parts


def _log_sigmoid(x):
    return jnp.minimum(x, 0.0) - jnp.log(1.0 + jnp.exp(-jnp.abs(x)))


def _silu(z):
    return z * (1.0 / (1.0 + jnp.exp(-z)))


def _sigmoid(z):
    return 1.0 / (1.0 + jnp.exp(-z))


def _fox_qk_kernel(tiles_per_seq, x_ref, g_ref, wq_ref, wk_ref, wf_ref, bf_ref,
                   place_ref, wt_ref, q_ref, k_ref, vf_ref, zf_ref, vd_ref, carry_ref):
    i = pl.program_id(0)

    @pl.when(i % tiles_per_seq == 0)
    def _():
        carry_ref[...] = jnp.zeros_like(carry_ref)

    h = _rms_rows(x_ref[...], g_ref[...], NORM_EPS).astype(BF16)

    def transposed(idx, ref):
        w = wt_ref[idx * WIDTH:(idx + 1) * WIDTH, :]
        ref[...] = lax.dot_general(w, h, _NT,
                                   preferred_element_type=F32).astype(BF16)

    lane = lax.broadcasted_iota(jnp.int32, (1, LANES), 1)
    head_lanes = lane < FOX_HEAD_DIM

    def head_slot(x, hh, extra):
        pair = x[:, (hh // 2) * LANES:(hh // 2 + 1) * LANES]
        if hh % 2:
            pair = pltpu.roll(pair, FOX_HEAD_DIM, axis=1)
        return jnp.where(head_lanes, pair, extra)

    f = jnp.dot(h, wf_ref[...], preferred_element_type=F32) + bf_ref[...]
    transposed(0, vf_ref)
    logf = _log_sigmoid(f)
    rows = lax.broadcasted_iota(jnp.int32, (TM, TM), 0)
    cols = lax.broadcasted_iota(jnp.int32, (TM, TM), 1)
    tri = (rows >= cols).astype(BF16)
    sums = jnp.dot(tri, jnp.concatenate(_split_bf16(logf), axis=1),
                   preferred_element_type=F32)
    c = carry_ref[...]
    for t in range(N_SPLIT):
        c = c + sums[:, t * LANES:(t + 1) * LANES]
    carry_ref[...] = c[TM - 1:TM, :]
    transposed(1, zf_ref)

    q = jnp.dot(h, wq_ref[...], preferred_element_type=F32) * QK_SCALE
    for hh in range(FOX_HEADS):
        lo = AUG0 + hh * N_SPLIT
        ones_lanes = jnp.where((lane >= lo) & (lane < lo + N_SPLIT), 1.0, 0.0)
        q_ref[hh] = head_slot(q, hh, ones_lanes).astype(BF16)

    neg_c = jnp.concatenate(_split_bf16(-LOG2E * c), axis=1)
    k_extra = jnp.dot(neg_c, place_ref[...], preferred_element_type=F32)
    transposed(2, vd_ref)
    k = jnp.dot(h, wk_ref[...], preferred_element_type=F32)
    for hh in range(FOX_HEADS):
        k_ref[hh] = head_slot(k, hh, k_extra).astype(BF16)


def _rope(x, cos, sin_signed):
    lane = lax.broadcasted_iota(jnp.int32, (1, LANES), 1) % DIFF_QK_DIM
    first_half = lane < DIFF_QK_DIM // 2
    outs = []
    for g in range(x.shape[1] // LANES):
        xg = x[:, g * LANES:(g + 1) * LANES]
        partner = jnp.where(first_half,
                            pltpu.roll(xg, LANES - DIFF_QK_DIM // 2, axis=1),
                            pltpu.roll(xg, DIFF_QK_DIM // 2, axis=1))
        outs.append(xg * cos + partner * sin_signed)
    return jnp.concatenate(outs, axis=1)


def _nat_kernel(x_ref, g_ref, w_ref, cos_ref, sin_ref, q_ref, k_ref, z_ref, gate_ref):
    h = _rms_rows(x_ref[...], g_ref[...], NORM_EPS).astype(BF16)
    cos = cos_ref[...]
    sin = sin_ref[...]
    q = jnp.dot(h, w_ref[:, 0:WIDTH], preferred_element_type=F32)
    q_ref[...] = (_rope(q, cos, sin) * QK_SCALE).astype(BF16)
    k = jnp.dot(h, w_ref[:, WIDTH:2 * WIDTH], preferred_element_type=F32)
    k_ref[...] = _rope(k, cos, sin).astype(BF16)
    z_ref[...] = jnp.dot(h, w_ref[:, 2 * WIDTH:3 * WIDTH],
                         preferred_element_type=F32).astype(BF16)
    gate_ref[...] = jnp.dot(h, w_ref[:, 3 * WIDTH:],
                            preferred_element_type=F32).astype(BF16)


def _two_chain_attention(i, q_fn, k_fn, vt_fn, rows, acc_ref, m_ref, al_ref, s_ref, bm_ref,
                         p1_ref):
    aug = rows + ONES_ROWS
    ones = jnp.ones((ONES_ROWS, TK), BF16)
    full = slice(0, TQ)
    upper = slice(TQ // 2, TQ)

    def produce(c, j, slot, causal, lanes=full, next_tile=False):
        q = q_fn(c, next_tile)[lanes, :]
        s_t = lax.dot_general(k_fn(c, j), q, _NT, preferred_element_type=F32)
        if causal:
            kpos = lax.broadcasted_iota(jnp.int32, s_t.shape, 0)
            qpos = lax.broadcasted_iota(jnp.int32, s_t.shape, 1)
            s_t = jnp.where(kpos - qpos <= TQ * i + lanes.start - TK * j, s_t, NEG_BIG)
        s_ref[slot, c, :, lanes] = s_t
        bm_ref[slot, c:c + 1, lanes] = jnp.max(s_t, axis=0, keepdims=True)

    def consume(c, slot, lanes):
        m_old = m_ref[c:c + 1, lanes]
        m_new = jnp.maximum(m_old, bm_ref[slot, c:c + 1, lanes])
        m_ref[c:c + 1, lanes] = m_new
        return jnp.exp2(s_ref[slot, c, :, lanes] - m_new).astype(BF16), jnp.exp2(m_old - m_new)

    def accumulate(c, j, lanes):
        vt = jnp.concatenate([vt_fn(c, j), ones], axis=0)
        sl = slice(c * aug, (c + 1) * aug)

        def update(alpha, p):
            acc_ref[sl, lanes] = alpha * acc_ref[sl, lanes] + jnp.dot(
                vt, p, preferred_element_type=F32)
        return update

    def iteration(j, slot, next_block, lanes=full):
        for c in range(2):
            produce(c, *next_block)
        accumulate(1, jnp.maximum(j - 1, 0), full)(al_ref[...], p1_ref[...])
        p0, alpha0 = consume(0, slot, lanes)
        accumulate(0, j, lanes)(alpha0, p0)
        p1, alpha1 = consume(1, slot, lanes)
        p1_ref[:, lanes] = p1
        al_ref[:, lanes] = alpha1

    m_ref[...] = jnp.full_like(m_ref, NEG_BIG)
    acc_ref[...] = jnp.zeros_like(acc_ref)
    p1_ref[...] = jnp.zeros_like(p1_ref)
    al_ref[...] = jnp.ones_like(al_ref)

    @pl.when(i == 0)
    def _():
        for c in range(2):
            produce(c, 0, 0, True)

    last = 2 * i + 1

    def pair(t, carry):
        iteration(2 * t, 0, (2 * t + 1, 1, False))
        iteration(2 * t + 1, 1, (2 * t + 2, 0, False))
        return carry

    lax.fori_loop(0, i - 1, pair, 0)

    @pl.when(i > 0)
    def _():
        iteration(last - 3, 0, (last - 2, 1, False))
        iteration(last - 2, 1, (last - 1, 0, True))

    iteration(last - 1, 0, (last, 1, True, upper))
    iteration(last, 1, (0, 0, False, full, True), upper)
    accumulate(1, last, upper)(al_ref[:, upper], p1_ref[:, upper])


def _kv_rows(j):
    return pl.ds(pl.multiple_of(j * TK, TK), TK)


def _normalized(acc_ref, c, rows):
    base = c * (rows + ONES_ROWS)
    return acc_ref[base:base + rows, :] * (1.0 / acc_ref[base + rows:base + rows + 1, :])


def _fox_attn_kernel(q_ref, qn_ref, k_ref, vt_ref, zt_ref, o_ref, acc_ref, *scratch):
    d = FOX_HEAD_DIM

    def q_fn(c, next_tile):
        return (qn_ref if next_tile else q_ref).at[c]

    def k_fn(c, j):
        return k_ref[c, _kv_rows(j), :]

    def vt_fn(c, j):
        return vt_ref[j, c * d:(c + 1) * d, :]

    _two_chain_attention(pl.program_id(2), q_fn, k_fn, vt_fn, d, acc_ref, *scratch)
    for c in range(2):
        sl = slice(c * d, (c + 1) * d)
        y = _normalized(acc_ref, c, d)
        for t in range(TQ // TM):
            z = zt_ref[t, sl, :].astype(F32)
            o_ref[t, sl, :] = (y[:, t * TM:(t + 1) * TM] * _silu(z)).astype(BF16)


def _diff_attn_kernel(lamv_ref, q_ref, qn_ref, k_ref, vt_ref, z_ref, g_ref, o_ref,
                      qm_ref, acc_ref, *scratch):
    lane = lax.broadcasted_iota(jnp.int32, (1, LANES), 1)
    for t, ref in enumerate((q_ref, qn_ref)):
        q = ref[...]
        zero = jnp.zeros_like(q)
        qm_ref[2 * t] = jnp.where(lane < DIFF_QK_DIM, q, zero)
        qm_ref[2 * t + 1] = jnp.where(lane >= DIFF_QK_DIM, q, zero)

    def q_fn(c, next_tile):
        return qm_ref.at[c + 2 * int(next_tile)]

    def k_fn(c, j):
        return k_ref[_kv_rows(j), :]

    def vt_fn(c, j):
        return vt_ref[j]

    _two_chain_attention(pl.program_id(2), q_fn, k_fn, vt_fn, DIFF_V_DIM, acc_ref, *scratch)

    lamv = lamv_ref[...]
    lam = (jnp.exp(jnp.sum(lamv[0:1] * lamv[1:2], axis=1, keepdims=True))
           - jnp.exp(jnp.sum(lamv[2:3] * lamv[3:4], axis=1, keepdims=True))
           + LAMBDA_INIT)
    o = (_normalized(acc_ref, 0, DIFF_V_DIM)
         - lam * _normalized(acc_ref, 1, DIFF_V_DIM))
    ms = jnp.mean(o * o, axis=0, keepdims=True)
    y = o * lax.rsqrt(ms + SUBLN_EPS) * g_ref[...] * (1.0 - LAMBDA_INIT)
    z = z_ref[...].astype(F32)
    o_ref[...] = (y.T * _silu(z)).astype(BF16)


def _merge_kernel(yat_ref, yb_ref, gate_ref, x_ref, wa_ref, wb_ref, wo_ref, g_ref, o_ref):
    ma = lax.dot_general(yat_ref[...], wa_ref[...], _TN, preferred_element_type=F32)
    mb = jnp.dot(yb_ref[...], wb_ref[...], preferred_element_type=F32)
    ga = _sigmoid(gate_ref[:, 0:D_MODEL].astype(F32))
    gb = _sigmoid(gate_ref[:, D_MODEL:].astype(F32))
    merged = (ga * ma + gb * mb).astype(BF16)
    y = jnp.dot(merged, wo_ref[...], preferred_element_type=F32)
    o_ref[...] = x_ref[...] + _rms_rows(y, g_ref[...], NORM_EPS)


_ATTN_PARAMS = pltpu.CompilerParams(
    dimension_semantics=("arbitrary", "arbitrary", "arbitrary"),
    vmem_limit_bytes=VMEM_LIMIT)


def _attn_scratch(rows):
    return [pltpu.VMEM((2 * (rows + ONES_ROWS), TQ), F32), pltpu.VMEM((2, TQ), F32),
            pltpu.VMEM((1, TQ), F32), pltpu.VMEM((2, 2, TK, TQ), F32),
            pltpu.VMEM((2, 2, TQ), F32), pltpu.VMEM((TK, TQ), BF16)]


def _placement():
    src = jnp.arange(N_SPLIT * LANES)
    term, head = src // LANES, src % LANES
    dst = AUG0 + head * N_SPLIT + term
    valid = head < FOX_HEADS
    cols = jnp.arange(LANES)
    return ((cols[None, :] == dst[:, None]) & valid[:, None]).astype(BF16)


def _rope_tables(seq):
    pos = jnp.arange(seq, dtype=F32)
    inv_freq = ROPE_THETA ** (-jnp.arange(0, DIFF_QK_DIM, 2, dtype=F32) / DIFF_QK_DIM)
    ang = pos[:, None] * inv_freq[None, :]
    cos, sin = jnp.cos(ang), jnp.sin(ang)
    reps = LANES // (DIFF_QK_DIM // 2)
    cos_t = jnp.tile(cos, (1, reps))
    sin_t = jnp.tile(jnp.concatenate([-sin, sin], axis=1), (1, reps // 2))
    return cos_t, sin_t


def kernel(x, g_pre, w_in, b_forget, lambda_q1, lambda_k1, lambda_q2, lambda_k2,
           g_subln, w_branch, w_out, g_post):
    batch, seq, d = x.shape
    assert d == D_MODEL and seq % TQ == 0 and TM == TK and TQ == 2 * TK
    n_rows = batch * seq
    n_tiles = n_rows // TM
    tiles_per_seq = seq // TM

    w = w_in[0]
    sizes = [WIDTH, WIDTH, WIDTH, FOX_HEADS, WIDTH, WIDTH, WIDTH, WIDTH, WIDTH, 2 * D_MODEL]
    offs = [0]
    for s in sizes:
        offs.append(offs[-1] + s)
    w_qa, w_ka, w_va, w_fa, w_za, w_qb, w_kb, w_vb, w_zb, w_gate = (
        w[:, offs[n]:offs[n + 1]] for n in range(len(sizes)))
    wq_f = w_qa.astype(BF16)
    wk_f = w_ka.astype(BF16)
    wf = jnp.pad(w_fa, ((0, 0), (0, LANES - FOX_HEADS))).astype(BF16)
    bf = jnp.pad(b_forget[0], (0, LANES - FOX_HEADS)).reshape(1, LANES).astype(F32)
    w_nat = jnp.concatenate([w_qb, w_kb, w_zb, w_gate], axis=1).astype(BF16)
    w_tr = jnp.concatenate([w_va, w_za, w_vb], axis=1).T.astype(BF16)
    g_pre2 = g_pre[0].reshape(1, D_MODEL)
    g_post2 = g_post[0].reshape(1, D_MODEL)
    g_sub2 = g_subln[0].reshape(DIFF_V_DIM, 1)
    lamv = jnp.pad(jnp.stack([lambda_q1[0], lambda_k1[0], lambda_q2[0], lambda_k2[0]]),
                   ((0, 0), (0, LANES - DIFF_QK_DIM))).astype(F32)
    wa = w_branch[0, 0].astype(BF16)
    wb = w_branch[0, 1].astype(BF16)
    wo = w_out[0].astype(BF16)
    cos_t, sin_t = _rope_tables(seq)
    x2 = x.reshape(n_rows, D_MODEL)

    x_spec = pl.BlockSpec((TM, D_MODEL), lambda i: (i, 0))
    g_spec = _const_spec((1, D_MODEL))

    qk_shape = jax.ShapeDtypeStruct((batch, FOX_HEADS, seq, LANES), BF16)
    qk_spec = pl.BlockSpec((None, FOX_HEADS, TM, LANES),
                           lambda i: (i // tiles_per_seq, 0, i % tiles_per_seq, 0))
    tr_shape = jax.ShapeDtypeStruct((n_tiles, WIDTH, TM), BF16)
    tr_spec = pl.BlockSpec((None, WIDTH, TM), lambda i: (i, 0, 0))
    q_fox, k_fox, v_fox_t, z_fox_t, v_diff_t = pl.pallas_call(
        functools.partial(_fox_qk_kernel, tiles_per_seq),
        out_shape=(qk_shape, qk_shape, tr_shape, tr_shape, tr_shape),
        grid=(n_tiles,),
        in_specs=[x_spec, g_spec,
                  _const_spec(wq_f.shape), _const_spec(wk_f.shape),
                  _const_spec(wf.shape), _const_spec(bf.shape),
                  _const_spec((N_SPLIT * LANES, LANES)),
                  _const_spec(w_tr.shape)],
        out_specs=(qk_spec, qk_spec, tr_spec, tr_spec, tr_spec),
        scratch_shapes=[pltpu.VMEM((1, LANES), F32)],
        compiler_params=pltpu.CompilerParams(
            dimension_semantics=("arbitrary",), vmem_limit_bytes=VMEM_LIMIT),
        name="fox_tr_proj",
    )(x2, g_pre2, wq_f, wk_f, wf, bf, _placement(), w_tr)
    v_fox_t = v_fox_t.reshape(batch, tiles_per_seq, WIDTH, TM)
    z_fox_t = z_fox_t.reshape(batch, tiles_per_seq, WIDTH, TM)
    v_diff_t = v_diff_t.reshape(batch, tiles_per_seq, WIDTH, TM)

    row_spec = lambda n: pl.BlockSpec((TM, n), lambda i: (i, 0))
    rope_spec = pl.BlockSpec((TM, LANES), lambda i: (i % tiles_per_seq, 0))
    q_diff, k_diff, z_diff, gates = pl.pallas_call(
        _nat_kernel,
        out_shape=(jax.ShapeDtypeStruct((n_rows, WIDTH), BF16),
                   jax.ShapeDtypeStruct((n_rows, WIDTH), BF16),
                   jax.ShapeDtypeStruct((n_rows, WIDTH), BF16),
                   jax.ShapeDtypeStruct((n_rows, 2 * D_MODEL), BF16)),
        grid=(n_tiles,),
        in_specs=[x_spec, g_spec, _const_spec(w_nat.shape), rope_spec, rope_spec],
        out_specs=(row_spec(WIDTH), row_spec(WIDTH), row_spec(WIDTH),
                   row_spec(2 * D_MODEL)),
        compiler_params=pltpu.CompilerParams(
            dimension_semantics=("parallel",), vmem_limit_bytes=VMEM_LIMIT),
        name="nat_proj",
    )(x2, g_pre2, w_nat, cos_t, sin_t)

    def next_tile(i):
        return jnp.minimum(i + 1, seq // TQ - 1)

    ya_t = pl.pallas_call(
        _fox_attn_kernel,
        out_shape=jax.ShapeDtypeStruct((batch, tiles_per_seq, WIDTH, TM), BF16),
        grid=(batch, FOX_HEADS // 2, seq // TQ),
        in_specs=[
            pl.BlockSpec((None, 2, TQ, LANES), lambda b, h, i: (b, h, i, 0)),
            pl.BlockSpec((None, 2, TQ, LANES), lambda b, h, i: (b, h, next_tile(i), 0)),
            pl.BlockSpec((None, 2, seq, LANES), lambda b, h, i: (b, h, 0, 0)),
            pl.BlockSpec((None, tiles_per_seq, 2 * FOX_HEAD_DIM, TK),
                         lambda b, h, i: (b, 0, h, 0)),
            pl.BlockSpec((None, TQ // TM, 2 * FOX_HEAD_DIM, TM),
                         lambda b, h, i: (b, i, h, 0)),
        ],
        out_specs=pl.BlockSpec((None, TQ // TM, 2 * FOX_HEAD_DIM, TM),
                               lambda b, h, i: (b, i, h, 0)),
        scratch_shapes=_attn_scratch(FOX_HEAD_DIM),
        compiler_params=_ATTN_PARAMS,
        name="fox_attn",
    )(q_fox, q_fox, k_fox, v_fox_t, z_fox_t)

    q_diff3 = q_diff.reshape(batch, seq, WIDTH)
    k_diff3 = k_diff.reshape(batch, seq, WIDTH)
    z_diff3 = z_diff.reshape(batch, seq, WIDTH)
    yb = pl.pallas_call(
        _diff_attn_kernel,
        out_shape=jax.ShapeDtypeStruct((batch, seq, WIDTH), BF16),
        grid=(batch, DIFF_HEADS, seq // TQ),
        in_specs=[
            pl.BlockSpec((4, LANES), lambda b, h, i: (0, 0)),
            pl.BlockSpec((None, TQ, LANES), lambda b, h, i: (b, i, h)),
            pl.BlockSpec((None, TQ, LANES), lambda b, h, i: (b, next_tile(i), h)),
            pl.BlockSpec((None, seq, LANES), lambda b, h, i: (b, 0, h)),
            pl.BlockSpec((None, tiles_per_seq, DIFF_V_DIM, TK),
                         lambda b, h, i: (b, 0, h, 0)),
            pl.BlockSpec((None, TQ, LANES), lambda b, h, i: (b, i, h)),
            pl.BlockSpec((DIFF_V_DIM, 1), lambda b, h, i: (0, 0)),
        ],
        out_specs=pl.BlockSpec((None, TQ, LANES), lambda b, h, i: (b, i, h)),
        scratch_shapes=[pltpu.VMEM((4, TQ, LANES), BF16)] + _attn_scratch(DIFF_V_DIM),
        compiler_params=_ATTN_PARAMS,
        name="diff_attn",
    )(lamv, q_diff3, q_diff3, k_diff3, v_diff_t, z_diff3, g_sub2)

    out = pl.pallas_call(
        _merge_kernel,
        out_shape=jax.ShapeDtypeStruct((n_rows, D_MODEL), F32),
        grid=(n_tiles,),
        in_specs=[
            pl.BlockSpec((None, WIDTH, TM), lambda i: (i, 0, 0)),
            row_spec(WIDTH), row_spec(2 * D_MODEL), x_spec,
            _const_spec(wa.shape), _const_spec(wb.shape), _const_spec(wo.shape),
            g_spec,
        ],
        out_specs=x_spec,
        compiler_params=pltpu.CompilerParams(
            dimension_semantics=("parallel",), vmem_limit_bytes=VMEM_LIMIT),
        name="merge_out",
    )(ya_t.reshape(n_tiles, WIDTH, TM), yb.reshape(n_rows, WIDTH), gates, x2,
      wa, wb, wo, g_post2)
    return out.reshape(batch, seq, D_MODEL)
```

```python
import functools
import math

import jax
import jax.numpy as jnp
from jax import lax
from jax.experimental import pallas as pl
from jax.experimental.pallas import tpu as pltpu

F32 = jnp.float32
BF16 = jnp.bfloat16

D_MODEL = 1024
FOX_HEADS = 8
FOX_HEAD_DIM = 64
DIFF_HEADS = 4
DIFF_QK_DIM = 64
DIFF_V_DIM = 128
WIDTH = 512
ROPE_THETA = 10000.0
NORM_EPS = 1e-6
SUBLN_EPS = 1e-5
LAMBDA_INIT = 0.8 - 0.6 * math.exp(-0.3 * 0)

LANES = 128
TM = 512
TQ = 1024
TK = 512
ONES_ROWS = 16
N_SPLIT = 3
AUG0 = FOX_HEAD_DIM
LOG2E = math.log2(math.e)
QK_SCALE = LOG2E / math.sqrt(FOX_HEAD_DIM)
NEG_BIG = -1e30
VMEM_LIMIT = 48 * 1024 * 1024

_NT = (((1,), (1,)), ((), ()))
_TN = (((0,), (0,)), ((), ()))


def _const_spec(shape):
    return pl.BlockSpec(shape, lambda *_: (0,) * len(shape),
                        pipeline_mode=pl.Buffered(1))


def _rms_rows(x, g, eps):
    ms = jnp.mean(x * x, axis=-1, keepdims=True)
    return x * lax.rsqrt(ms + eps) * g


def _split_bf16(x):
    parts = []
    r = x
    for _ in range(N_SPLIT - 1):
        p = r.astype(BF16)
        parts.append(p)
        r = r - p.astype(F32)
    parts.append(r.astype(BF16))
    return parts


def _log_sigmoid(x):
    return jnp.minimum(x, 0.0) - jnp.log(1.0 + jnp.exp(-jnp.abs(x)))


def _silu(z):
    return z * (1.0 / (1.0 + jnp.exp(-z)))


def _sigmoid(z):
    return 1.0 / (1.0 + jnp.exp(-z))


def _fox_qk_kernel(tiles_per_seq, x_ref, g_ref, wq_ref, wk_ref, wf_ref, bf_ref,
                   place_ref, wt_ref, q_ref, k_ref, vf_ref, zf_ref, vd_ref, carry_ref):
    i = pl.program_id(0)

    @pl.when(i % tiles_per_seq == 0)
    def _():
        carry_ref[...] = jnp.zeros_like(carry_ref)

    h = _rms_rows(x_ref[...], g_ref[...], NORM_EPS).astype(BF16)

    def transposed(idx, ref):
        w = wt_ref[idx * WIDTH:(idx + 1) * WIDTH, :]
        ref[...] = lax.dot_general(w, h, _NT,
                                   preferred_element_type=F32).astype(BF16)

    lane = lax.broadcasted_iota(jnp.int32, (1, LANES), 1)
    head_lanes = lane < FOX_HEAD_DIM

    def head_slot(x, hh, extra):
        pair = x[:, (hh // 2) * LANES:(hh // 2 + 1) * LANES]
        if hh % 2:
            pair = pltpu.roll(pair, FOX_HEAD_DIM, axis=1)
        return jnp.where(head_lanes, pair, extra)

    f = jnp.dot(h, wf_ref[...], preferred_element_type=F32) + bf_ref[...]
    transposed(0, vf_ref)
    logf = _log_sigmoid(f)
    rows = lax.broadcasted_iota(jnp.int32, (TM, TM), 0)
    cols = lax.broadcasted_iota(jnp.int32, (TM, TM), 1)
    tri = (rows >= cols).astype(BF16)
    sums = jnp.dot(tri, jnp.concatenate(_split_bf16(logf), axis=1),
                   preferred_element_type=F32)
    c = carry_ref[...]
    for t in range(N_SPLIT):
        c = c + sums[:, t * LANES:(t + 1) * LANES]
    carry_ref[...] = c[TM - 1:TM, :]
    transposed(1, zf_ref)

    q = jnp.dot(h, wq_ref[...], preferred_element_type=F32) * QK_SCALE
    for hh in range(FOX_HEADS):
        lo = AUG0 + hh * N_SPLIT
        ones_lanes = jnp.where((lane >= lo) & (lane < lo + N_SPLIT), 1.0, 0.0)
        q_ref[hh] = head_slot(q, hh, ones_lanes).astype(BF16)

    neg_c = jnp.concatenate(_split_bf16(-LOG2E * c), axis=1)
    k_extra = jnp.dot(neg_c, place_ref[...], preferred_element_type=F32)
    transposed(2, vd_ref)
    k = jnp.dot(h, wk_ref[...], preferred_element_type=F32)
    for hh in range(FOX_HEADS):
        k_ref[hh] = head_slot(k, hh, k_extra).astype(BF16)


def _rope(x, cos, sin_signed):
    lane = lax.broadcasted_iota(jnp.int32, (1, LANES), 1) % DIFF_QK_DIM
    first_half = lane < DIFF_QK_DIM // 2
    outs = []
    for g in range(x.shape[1] // LANES):
        xg = x[:, g * LANES:(g + 1) * LANES]
        partner = jnp.where(first_half,
                            pltpu.roll(xg, LANES - DIFF_QK_DIM // 2, axis=1),
                            pltpu.roll(xg, DIFF_QK_DIM // 2, axis=1))
        outs.append(xg * cos + partner * sin_signed)
    return jnp.concatenate(outs, axis=1)


def _nat_kernel(x_ref, g_ref, w_ref, cos_ref, sin_ref, q_ref, k_ref, z_ref, gate_ref):
    h = _rms_rows(x_ref[...], g_ref[...], NORM_EPS).astype(BF16)
    cos = cos_ref[...]
    sin = sin_ref[...]
    q = jnp.dot(h, w_ref[:, 0:WIDTH], preferred_element_type=F32)
    q_ref[...] = (_rope(q, cos, sin) * QK_SCALE).astype(BF16)
    k = jnp.dot(h, w_ref[:, WIDTH:2 * WIDTH], preferred_element_type=F32)
    k_ref[...] = _rope(k, cos, sin).astype(BF16)
    z_ref[...] = jnp.dot(h, w_ref[:, 2 * WIDTH:3 * WIDTH],
                         preferred_element_type=F32).astype(BF16)
    gate_ref[...] = jnp.dot(h, w_ref[:, 3 * WIDTH:],
                            preferred_element_type=F32).astype(BF16)


def _two_chain_attention(i, q_fn, k_fn, vt_fn, rows, acc_ref, m_ref, al_ref, s_ref, bm_ref,
                         p1_ref):
    aug = rows + ONES_ROWS
    ones = jnp.ones((ONES_ROWS, TK), BF16)
    full = slice(0, TQ)
    upper = slice(TQ // 2, TQ)

    def produce(c, j, slot, causal, lanes=full, next_tile=False):
        q = q_fn(c, next_tile)[lanes, :]
        s_t = lax.dot_general(k_fn(c, j), q, _NT, preferred_element_type=F32)
        if causal:
            kpos = lax.broadcasted_iota(jnp.int32, s_t.shape, 0)
            qpos = lax.broadcasted_iota(jnp.int32, s_t.shape, 1)
            s_t = jnp.where(kpos - qpos <= TQ * i + lanes.start - TK * j, s_t, NEG_BIG)
        s_ref[slot, c, :, lanes] = s_t
        bm_ref[slot, c:c + 1, lanes] = jnp.max(s_t, axis=0, keepdims=True)

    def consume(c, slot, lanes):
        m_old = m_ref[c:c + 1, lanes]
        m_new = jnp.maximum(m_old, bm_ref[slot, c:c + 1, lanes])
        m_ref[c:c + 1, lanes] = m_new
        return jnp.exp2(s_ref[slot, c, :, lanes] - m_new).astype(BF16), jnp.exp2(m_old - m_new)

    def accumulate(c, j, lanes):
        vt = jnp.concatenate([vt_fn(c, j), ones], axis=0)
        sl = slice(c * aug, (c + 1) * aug)

        def update(alpha, p):
            acc_ref[sl, lanes] = alpha * acc_ref[sl, lanes] + jnp.dot(
                vt, p, preferred_element_type=F32)
        return update

    def iteration(j, slot, next_block, lanes=full):
        for c in range(2):
            produce(c, *next_block)
        accumulate(1, jnp.maximum(j - 1, 0), full)(al_ref[...], p1_ref[...])
        p0, alpha0 = consume(0, slot, lanes)
        accumulate(0, j, lanes)(alpha0, p0)
        p1, alpha1 = consume(1, slot, lanes)
        p1_ref[:, lanes] = p1
        al_ref[:, lanes] = alpha1

    m_ref[...] = jnp.full_like(m_ref, NEG_BIG)
    acc_ref[...] = jnp.zeros_like(acc_ref)
    p1_ref[...] = jnp.zeros_like(p1_ref)
    al_ref[...] = jnp.ones_like(al_ref)

    @pl.when(i == 0)
    def _():
        for c in range(2):
            produce(c, 0, 0, True)

    last = 2 * i + 1

    def pair(t, carry):
        iteration(2 * t, 0, (2 * t + 1, 1, False))
        iteration(2 * t + 1, 1, (2 * t + 2, 0, False))
        return carry

    lax.fori_loop(0, i - 1, pair, 0)

    @pl.when(i > 0)
    def _():
        iteration(last - 3, 0, (last - 2, 1, False))
        iteration(last - 2, 1, (last - 1, 0, True))

    iteration(last - 1, 0, (last, 1, True, upper))
    iteration(last, 1, (0, 0, False, full, True), upper)
    accumulate(1, last, upper)(al_ref[:, upper], p1_ref[:, upper])


def _kv_rows(j):
    return pl.ds(pl.multiple_of(j * TK, TK), TK)


def _normalized(acc_ref, c, rows):
    base = c * (rows + ONES_ROWS)
    return acc_ref[base:base + rows, :] * (1.0 / acc_ref[base + rows:base + rows + 1, :])


def _fox_attn_kernel(q_ref, qn_ref, k_ref, vt_ref, zt_ref, o_ref, acc_ref, *scratch):
    d = FOX_HEAD_DIM

    def q_fn(c, next_tile):
        return (qn_ref if next_tile else q_ref).at[c]

    def k_fn(c, j):
        return k_ref[c, _kv_rows(j), :]

    def vt_fn(c, j):
        return vt_ref[j, c * d:(c + 1) * d, :]

    _two_chain_attention(pl.program_id(2), q_fn, k_fn, vt_fn, d, acc_ref, *scratch)
    for c in range(2):
        sl = slice(c * d, (c + 1) * d)
        y = _normalized(acc_ref, c, d)
        for t in range(TQ // TM):
            z = zt_ref[t, sl, :].astype(F32)
            o_ref[t, sl, :] = (y[:, t * TM:(t + 1) * TM] * _silu(z)).astype(BF16)


def _diff_attn_kernel(lamv_ref, q_ref, qn_ref, k_ref, vt_ref, z_ref, g_ref, o_ref,
                      qm_ref, acc_ref, *scratch):
    lane = lax.broadcasted_iota(jnp.int32, (1, LANES), 1)
    for t, ref in enumerate((q_ref, qn_ref)):
        q = ref[...]
        zero = jnp.zeros_like(q)
        qm_ref[2 * t] = jnp.where(lane < DIFF_QK_DIM, q, zero)
        qm_ref[2 * t + 1] = jnp.where(lane >= DIFF_QK_DIM, q, zero)

    def q_fn(c, next_tile):
        return qm_ref.at[c + 2 * int(next_tile)]

    def k_fn(c, j):
        return k_ref[_kv_rows(j), :]

    def vt_fn(c, j):
        return vt_ref[j]

    _two_chain_attention(pl.program_id(2), q_fn, k_fn, vt_fn, DIFF_V_DIM, acc_ref, *scratch)

    lamv = lamv_ref[...]
    lam = (jnp.exp(jnp.sum(lamv[0:1] * lamv[1:2], axis=1, keepdims=True))
           - jnp.exp(jnp.sum(lamv[2:3] * lamv[3:4], axis=1, keepdims=True))
           + LAMBDA_INIT)
    o = (_normalized(acc_ref, 0, DIFF_V_DIM)
         - lam * _normalized(acc_ref, 1, DIFF_V_DIM))
    ms = jnp.mean(o * o, axis=0, keepdims=True)
    y = o * lax.rsqrt(ms + SUBLN_EPS) * g_ref[...] * (1.0 - LAMBDA_INIT)
    z = z_ref[...].astype(F32)
    o_ref[...] = (y.T * _silu(z)).astype(BF16)


def _merge_kernel(yat_ref, yb_ref, gate_ref, x_ref, wa_ref, wb_ref, wo_ref, g_ref, o_ref,
                  y_ref):
    @pl.when(pl.program_id(0) == 0)
    def _():
        y_ref[...] = jnp.zeros_like(y_ref)

    o_ref[...] = x_ref[...] + _rms_rows(y_ref[...], g_ref[...], NORM_EPS)

    ma = lax.dot_general(yat_ref[...], wa_ref[...], _TN, preferred_element_type=F32)
    mb = jnp.dot(yb_ref[...], wb_ref[...], preferred_element_type=F32)
    ga = _sigmoid(gate_ref[:, 0:D_MODEL].astype(F32))
    gb = _sigmoid(gate_ref[:, D_MODEL:].astype(F32))
    merged = (ga * ma + gb * mb).astype(BF16)
    y_ref[...] = jnp.dot(merged, wo_ref[...], preferred_element_type=F32)


_ATTN_PARAMS = pltpu.CompilerParams(
    dimension_semantics=("arbitrary", "arbitrary", "arbitrary"),
    vmem_limit_bytes=VMEM_LIMIT)


def _attn_scratch(rows):
    return [pltpu.VMEM((2 * (rows + ONES_ROWS), TQ), F32), pltpu.VMEM((2, TQ), F32),
            pltpu.VMEM((1, TQ), F32), pltpu.VMEM((2, 2, TK, TQ), F32),
            pltpu.VMEM((2, 2, TQ), F32), pltpu.VMEM((TK, TQ), BF16)]


def _placement():
    src = jnp.arange(N_SPLIT * LANES)
    term, head = src // LANES, src % LANES
    dst = AUG0 + head * N_SPLIT + term
    valid = head < FOX_HEADS
    cols = jnp.arange(LANES)
    return ((cols[None, :] == dst[:, None]) & valid[:, None]).astype(BF16)


def _rope_tables(seq):
    pos = jnp.arange(seq, dtype=F32)
    inv_freq = ROPE_THETA ** (-jnp.arange(0, DIFF_QK_DIM, 2, dtype=F32) / DIFF_QK_DIM)
    ang = pos[:, None] * inv_freq[None, :]
    cos, sin = jnp.cos(ang), jnp.sin(ang)
    reps = LANES // (DIFF_QK_DIM // 2)
    cos_t = jnp.tile(cos, (1, reps))
    sin_t = jnp.tile(jnp.concatenate([-sin, sin], axis=1), (1, reps // 2))
    return cos_t, sin_t


def kernel(x, g_pre, w_in, b_forget, lambda_q1, lambda_k1, lambda_q2, lambda_k2,
           g_subln, w_branch, w_out, g_post):
    batch, seq, d = x.shape
    assert d == D_MODEL and seq % TQ == 0 and TM == TK and TQ == 2 * TK
    n_rows = batch * seq
    n_tiles = n_rows // TM
    tiles_per_seq = seq // TM

    w = w_in[0]
    sizes = [WIDTH, WIDTH, WIDTH, FOX_HEADS, WIDTH, WIDTH, WIDTH, WIDTH, WIDTH, 2 * D_MODEL]
    offs = [0]
    for s in sizes:
        offs.append(offs[-1] + s)
    w_qa, w_ka, w_va, w_fa, w_za, w_qb, w_kb, w_vb, w_zb, w_gate = (
        w[:, offs[n]:offs[n + 1]] for n in range(len(sizes)))
    wq_f = w_qa.astype(BF16)
    wk_f = w_ka.astype(BF16)
    wf = jnp.pad(w_fa, ((0, 0), (0, LANES - FOX_HEADS))).astype(BF16)
    bf = jnp.pad(b_forget[0], (0, LANES - FOX_HEADS)).reshape(1, LANES).astype(F32)
    w_nat = jnp.concatenate([w_qb, w_kb, w_zb, w_gate], axis=1).astype(BF16)
    w_tr = jnp.concatenate([w_va, w_za, w_vb], axis=1).T.astype(BF16)
    g_pre2 = g_pre[0].reshape(1, D_MODEL)
    g_post2 = g_post[0].reshape(1, D_MODEL)
    g_sub2 = g_subln[0].reshape(DIFF_V_DIM, 1)
    lamv = jnp.pad(jnp.stack([lambda_q1[0], lambda_k1[0], lambda_q2[0], lambda_k2[0]]),
                   ((0, 0), (0, LANES - DIFF_QK_DIM))).astype(F32)
    wa = w_branch[0, 0].astype(BF16)
    wb = w_branch[0, 1].astype(BF16)
    wo = w_out[0].astype(BF16)
    cos_t, sin_t = _rope_tables(seq)
    x2 = x.reshape(n_rows, D_MODEL)

    x_spec = pl.BlockSpec((TM, D_MODEL), lambda i: (i, 0))
    g_spec = _const_spec((1, D_MODEL))

    qk_shape = jax.ShapeDtypeStruct((batch, FOX_HEADS, seq, LANES), BF16)
    qk_spec = pl.BlockSpec((None, FOX_HEADS, TM, LANES),
                           lambda i: (i // tiles_per_seq, 0, i % tiles_per_seq, 0))
    tr_shape = jax.ShapeDtypeStruct((n_tiles, WIDTH, TM), BF16)
    tr_spec = pl.BlockSpec((None, WIDTH, TM), lambda i: (i, 0, 0))
    q_fox, k_fox, v_fox_t, z_fox_t, v_diff_t = pl.pallas_call(
        functools.partial(_fox_qk_kernel, tiles_per_seq),
        out_shape=(qk_shape, qk_shape, tr_shape, tr_shape, tr_shape),
        grid=(n_tiles,),
        in_specs=[x_spec, g_spec,
                  _const_spec(wq_f.shape), _const_spec(wk_f.shape),
                  _const_spec(wf.shape), _const_spec(bf.shape),
                  _const_spec((N_SPLIT * LANES, LANES)),
                  _const_spec(w_tr.shape)],
        out_specs=(qk_spec, qk_spec, tr_spec, tr_spec, tr_spec),
        scratch_shapes=[pltpu.VMEM((1, LANES), F32)],
        compiler_params=pltpu.CompilerParams(
            dimension_semantics=("arbitrary",), vmem_limit_bytes=VMEM_LIMIT),
        name="fox_tr_proj",
    )(x2, g_pre2, wq_f, wk_f, wf, bf, _placement(), w_tr)
    v_fox_t = v_fox_t.reshape(batch, tiles_per_seq, WIDTH, TM)
    z_fox_t = z_fox_t.reshape(batch, tiles_per_seq, WIDTH, TM)
    v_diff_t = v_diff_t.reshape(batch, tiles_per_seq, WIDTH, TM)

    row_spec = lambda n: pl.BlockSpec((TM, n), lambda i: (i, 0))
    rope_spec = pl.BlockSpec((TM, LANES), lambda i: (i % tiles_per_seq, 0))
    q_diff, k_diff, z_diff, gates = pl.pallas_call(
        _nat_kernel,
        out_shape=(jax.ShapeDtypeStruct((n_rows, WIDTH), BF16),
                   jax.ShapeDtypeStruct((n_rows, WIDTH), BF16),
                   jax.ShapeDtypeStruct((n_rows, WIDTH), BF16),
                   jax.ShapeDtypeStruct((n_rows, 2 * D_MODEL), BF16)),
        grid=(n_tiles,),
        in_specs=[x_spec, g_spec, _const_spec(w_nat.shape), rope_spec, rope_spec],
        out_specs=(row_spec(WIDTH), row_spec(WIDTH), row_spec(WIDTH),
                   row_spec(2 * D_MODEL)),
        compiler_params=pltpu.CompilerParams(
            dimension_semantics=("parallel",), vmem_limit_bytes=VMEM_LIMIT),
        name="nat_proj",
    )(x2, g_pre2, w_nat, cos_t, sin_t)

    def next_tile(i):
        return jnp.minimum(i + 1, seq // TQ - 1)

    ya_t = pl.pallas_call(
        _fox_attn_kernel,
        out_shape=jax.ShapeDtypeStruct((batch, tiles_per_seq, WIDTH, TM), BF16),
        grid=(batch, FOX_HEADS // 2, seq // TQ),
        in_specs=[
            pl.BlockSpec((None, 2, TQ, LANES), lambda b, h, i: (b, h, i, 0)),
            pl.BlockSpec((None, 2, TQ, LANES), lambda b, h, i: (b, h, next_tile(i), 0)),
            pl.BlockSpec((None, 2, seq, LANES), lambda b, h, i: (b, h, 0, 0)),
            pl.BlockSpec((None, tiles_per_seq, 2 * FOX_HEAD_DIM, TK),
                         lambda b, h, i: (b, 0, h, 0)),
            pl.BlockSpec((None, TQ // TM, 2 * FOX_HEAD_DIM, TM),
                         lambda b, h, i: (b, i, h, 0)),
        ],
        out_specs=pl.BlockSpec((None, TQ // TM, 2 * FOX_HEAD_DIM, TM),
                               lambda b, h, i: (b, i, h, 0)),
        scratch_shapes=_attn_scratch(FOX_HEAD_DIM),
        compiler_params=_ATTN_PARAMS,
        name="fox_attn",
    )(q_fox, q_fox, k_fox, v_fox_t, z_fox_t)

    q_diff3 = q_diff.reshape(batch, seq, WIDTH)
    k_diff3 = k_diff.reshape(batch, seq, WIDTH)
    z_diff3 = z_diff.reshape(batch, seq, WIDTH)
    yb = pl.pallas_call(
        _diff_attn_kernel,
        out_shape=jax.ShapeDtypeStruct((batch, seq, WIDTH), BF16),
        grid=(batch, DIFF_HEADS, seq // TQ),
        in_specs=[
            pl.BlockSpec((4, LANES), lambda b, h, i: (0, 0)),
            pl.BlockSpec((None, TQ, LANES), lambda b, h, i: (b, i, h)),
            pl.BlockSpec((None, TQ, LANES), lambda b, h, i: (b, next_tile(i), h)),
            pl.BlockSpec((None, seq, LANES), lambda b, h, i: (b, 0, h)),
            pl.BlockSpec((None, tiles_per_seq, DIFF_V_DIM, TK),
                         lambda b, h, i: (b, 0, h, 0)),
            pl.BlockSpec((None, TQ, LANES), lambda b, h, i: (b, i, h)),
            pl.BlockSpec((DIFF_V_DIM, 1), lambda b, h, i: (0, 0)),
        ],
        out_specs=pl.BlockSpec((None, TQ, LANES), lambda b, h, i: (b, i, h)),
        scratch_shapes=[pltpu.VMEM((4, TQ, LANES), BF16)] + _attn_scratch(DIFF_V_DIM),
        compiler_params=_ATTN_PARAMS,
        name="diff_attn",
    )(lamv, q_diff3, q_diff3, k_diff3, v_diff_t, z_diff3, g_sub2)

    def cur_tile(i):
        return jnp.minimum(i, n_tiles - 1)

    def prev_tile(i):
        return jnp.maximum(i - 1, 0)

    out = pl.pallas_call(
        _merge_kernel,
        out_shape=jax.ShapeDtypeStruct((n_rows, D_MODEL), F32),
        grid=(n_tiles + 1,),
        in_specs=[
            pl.BlockSpec((None, WIDTH, TM), lambda i: (cur_tile(i), 0, 0)),
            pl.BlockSpec((TM, WIDTH), lambda i: (cur_tile(i), 0)),
            pl.BlockSpec((TM, 2 * D_MODEL), lambda i: (cur_tile(i), 0)),
            pl.BlockSpec((TM, D_MODEL), lambda i: (prev_tile(i), 0)),
            _const_spec(wa.shape), _const_spec(wb.shape), _const_spec(wo.shape),
            g_spec,
        ],
        out_specs=pl.BlockSpec((TM, D_MODEL), lambda i: (prev_tile(i), 0)),
        scratch_shapes=[pltpu.VMEM((TM, D_MODEL), F32)],
        compiler_params=pltpu.CompilerParams(
            dimension_semantics=("arbitrary",), vmem_limit_bytes=VMEM_LIMIT),
        name="merge_out",
    )(ya_t.reshape(n_tiles, WIDTH, TM), yb.reshape(n_rows, WIDTH), gates, x2,
      wa, wb, wo, g_post2)
    return out.reshape(batch, seq, D_MODEL)
```
